```python
import jax, jax.numpy as jnp
from jax import lax
import numpy as np

D_MODEL = 1024
BATCH = 4
SEQ = 8192
DEPTH = 1

D_MIX = D_MODEL
HEAD_DIM = 64
ATTN_WIDTH = D_MIX // 2
N_Q_HEADS = ATTN_WIDTH // HEAD_DIM
N_KV_HEADS = 2
Q_PER_KV = N_Q_HEADS // N_KV_HEADS
KV_WIDTH = N_KV_HEADS * HEAD_DIM
WINDOW = 128
ATTN_BLOCK = 128
GMLP_WIDTH = D_MIX - ATTN_WIDTH
N_GMLP_HEADS = GMLP_WIDTH // HEAD_DIM
GMLP_CHUNK = 128
IN_PROJ = ATTN_WIDTH + 2 * KV_WIDTH + 2 * GMLP_WIDTH
N_EXPERT_GROUPS = 4
EXPERTS_PER_GROUP = 8
N_EXPERTS = N_EXPERT_GROUPS * EXPERTS_PER_GROUP
TOP_K_IN_GROUP = 2
D_EXPERT = 256
RMS_EPS = 1e-6
LN_EPS = 1e-5
NEG_INF = -1e30

kernel_name = "hymba_swa_sink_gmlp_hiermoe_block"


def rmsnorm(x, g):
    xf = x.astype(jnp.float32)
    y = xf * lax.rsqrt(jnp.mean(xf * xf, axis=-1, keepdims=True) + RMS_EPS)
    return (y * g.astype(jnp.float32)).astype(x.dtype)


def layernorm(x, g, b):
    xf = x.astype(jnp.float32)
    mu = jnp.mean(xf, axis=-1, keepdims=True)
    var = jnp.mean(jnp.square(xf - mu), axis=-1, keepdims=True)
    y = (xf - mu) * lax.rsqrt(var + LN_EPS)
    return (y * g.astype(jnp.float32) + b.astype(jnp.float32)).astype(x.dtype)


def sliding_window_attention(q, k, v, sinks):
    B, S = q.shape[0], q.shape[1]
    nb = S // ATTN_BLOCK
    qb = q.reshape(B, nb, ATTN_BLOCK, N_KV_HEADS, Q_PER_KV, HEAD_DIM)
    kb = k.reshape(B, nb, ATTN_BLOCK, N_KV_HEADS, HEAD_DIM)
    vb = v.reshape(B, nb, ATTN_BLOCK, N_KV_HEADS, HEAD_DIM)
    pad = ((0, 0), (1, 0), (0, 0), (0, 0), (0, 0))
    k_band = jnp.concatenate([jnp.pad(kb[:, :-1], pad), kb], axis=2)
    v_band = jnp.concatenate([jnp.pad(vb[:, :-1], pad), vb], axis=2)
    scale = HEAD_DIM ** -0.5
    scores = jnp.einsum('bnqkgd,bnskd->bnkgqs', qb, k_band).astype(jnp.float32) * scale
    q_loc = jnp.arange(ATTN_BLOCK)[:, None]
    k_off = jnp.arange(2 * ATTN_BLOCK)[None, :] - ATTN_BLOCK
    diff = q_loc - k_off
    in_window = (diff >= 0) & (diff < WINDOW)
    k_abs = jnp.arange(nb)[:, None, None] * ATTN_BLOCK + k_off[None]
    mask = in_window[None] & (k_abs >= 0)
    scores = jnp.where(mask[None, :, None, None], scores, NEG_INF)
    sink = sinks.astype(jnp.float32).reshape(N_KV_HEADS, Q_PER_KV)[None, None, :, :, None, None]
    m = jnp.maximum(jnp.max(scores, axis=-1, keepdims=True), sink)
    p = jnp.exp(scores - m)
    denom = jnp.sum(p, axis=-1, keepdims=True) + jnp.exp(sink - m)
    probs = (p / denom).astype(v.dtype)
    out = jnp.einsum('bnkgqs,bnskd->bnqkgd', probs, v_band)
    return out.reshape(B, S, ATTN_WIDTH)


def chunked_spatial_gating(u, v, w_s, b_s, ln_g, ln_b):
    B, S = u.shape[0], u.shape[1]
    nc = S // GMLP_CHUNK
    u = jax.nn.gelu(u, approximate=False)
    v = layernorm(jax.nn.gelu(v, approximate=False), ln_g, ln_b)
    vc = v.reshape(B, nc, GMLP_CHUNK, N_GMLP_HEADS, HEAD_DIM)
    causal = jnp.tril(jnp.ones((GMLP_CHUNK, GMLP_CHUNK), dtype=bool))
    w = jnp.where(causal[None], w_s, 0).astype(v.dtype)
    s = jnp.einsum('htc,bnchd->bnthd', w, vc) + b_s.T.astype(v.dtype)[None, None, :, :, None]
    return u * s.reshape(B, S, GMLP_WIDTH)


def hierarchical_moe(x, w_gr, b_gr, w_er, b_er, w_gate, w_up, w_down):
    B, S, D = x.shape
    t = x.reshape(B * S, D)
    g_logits = (t @ w_gr).astype(jnp.float32) + b_gr.astype(jnp.float32)
    g_prob = jax.nn.softmax(g_logits, axis=-1)
    g_idx = jnp.argmax(g_logits, axis=-1)
    g_w = jnp.take_along_axis(g_prob, g_idx[:, None], axis=-1)
    e_all = jnp.einsum('td,gde->tge', t, w_er).astype(jnp.float32) + b_er.astype(jnp.float32)
    e_logits = jnp.take_along_axis(e_all, g_idx[:, None, None], axis=1)[:, 0]
    top_v, top_i = lax.top_k(e_logits, TOP_K_IN_GROUP)
    e_w = jax.nn.softmax(top_v, axis=-1)
    within = jnp.sum(jax.nn.one_hot(top_i, EXPERTS_PER_GROUP, dtype=jnp.float32) * e_w[..., None], axis=1)
    gates = jax.nn.one_hot(g_idx, N_EXPERT_GROUPS, dtype=jnp.float32)[:, :, None] * within[:, None, :] * g_w[:, :, None]
    gates = gates.reshape(B * S, N_EXPERTS).astype(t.dtype)
    out = jnp.zeros_like(t)
    for e in range(N_EXPERTS):
        hid = jax.nn.silu(t @ w_gate[e]) * (t @ w_up[e])
        out = out + gates[:, e:e + 1] * (hid @ w_down[e])
    return out.reshape(B, S, D)


def setup_inputs(seed: int = 0) -> dict:
    key = jax.random.key(seed)
    ks = jax.random.split(key, 20)
    f32 = jnp.float32
    nrm = lambda k, shape, s: (jax.random.normal(k, shape, f32) * s)
    L = DEPTH
    return {
        "x": nrm(ks[0], (BATCH, SEQ, D_MODEL), 1.0),
        "mix_norm_g": 1.0 + nrm(ks[1], (L, D_MODEL), 0.01),
        "w_in": nrm(ks[2], (L, D_MODEL, IN_PROJ), D_MODEL ** -0.5),
        "attn_sinks": nrm(ks[3], (L, N_Q_HEADS), 1.0),
        "w_spatial": nrm(ks[4], (L, N_GMLP_HEADS, GMLP_CHUNK, GMLP_CHUNK), GMLP_CHUNK ** -0.5),
        "b_spatial": 1.0 + nrm(ks[5], (L, N_GMLP_HEADS, GMLP_CHUNK), 0.1),
        "gmlp_ln_g": 1.0 + nrm(ks[6], (L, GMLP_WIDTH), 0.01),
        "gmlp_ln_b": nrm(ks[7], (L, GMLP_WIDTH), 0.01),
        "attn_out_g": 1.0 + nrm(ks[8], (L, ATTN_WIDTH), 0.01),
        "gmlp_out_g": 1.0 + nrm(ks[9], (L, GMLP_WIDTH), 0.01),
        "w_out": nrm(ks[10], (L, D_MIX, D_MODEL), D_MIX ** -0.5),
        "ffn_norm_g": 1.0 + nrm(ks[11], (L, D_MODEL), 0.01),
        "w_group_router": nrm(ks[12], (L, D_MODEL, N_EXPERT_GROUPS), D_MODEL ** -0.5),
        "b_group_router": nrm(ks[13], (L, N_EXPERT_GROUPS), 0.01),
        "w_expert_router": nrm(ks[14], (L, N_EXPERT_GROUPS, D_MODEL, EXPERTS_PER_GROUP), D_MODEL ** -0.5),
        "b_expert_router": nrm(ks[15], (L, N_EXPERT_GROUPS, EXPERTS_PER_GROUP), 0.01),
        "w_gate": nrm(ks[16], (L, N_EXPERTS, D_MODEL, D_EXPERT), D_MODEL ** -0.5),
        "w_up": nrm(ks[17], (L, N_EXPERTS, D_MODEL, D_EXPERT), D_MODEL ** -0.5),
        "w_down": nrm(ks[18], (L, N_EXPERTS, D_EXPERT, D_MODEL), D_EXPERT ** -0.5),
        "final_norm_g": 1.0 + nrm(ks[19], (D_MODEL,), 0.01),
    }


def reference(x, mix_norm_g, w_in, attn_sinks, w_spatial, b_spatial, gmlp_ln_g, gmlp_ln_b,
              attn_out_g, gmlp_out_g, w_out, ffn_norm_g, w_group_router, b_group_router,
              w_expert_router, b_expert_router, w_gate, w_up, w_down, final_norm_g):
    B, S = x.shape[0], x.shape[1]
    splits = [ATTN_WIDTH, ATTN_WIDTH + KV_WIDTH, ATTN_WIDTH + 2 * KV_WIDTH,
              ATTN_WIDTH + 2 * KV_WIDTH + GMLP_WIDTH]
    h = x
    for l in range(DEPTH):
        hn = rmsnorm(h, mix_norm_g[l])
        proj = hn @ w_in[l]
        q, k, v, gu, gv = jnp.split(proj, splits, axis=-1)
        q = q.reshape(B, S, N_Q_HEADS, HEAD_DIM)
        k = k.reshape(B, S, N_KV_HEADS, HEAD_DIM)
        v = v.reshape(B, S, N_KV_HEADS, HEAD_DIM)
        a = sliding_window_attention(q, k, v, attn_sinks[l])
        g = chunked_spatial_gating(gu, gv, w_spatial[l], b_spatial[l], gmlp_ln_g[l], gmlp_ln_b[l])
        mixed = jnp.concatenate([rmsnorm(a, attn_out_g[l]), rmsnorm(g, gmlp_out_g[l])], axis=-1)
        h = h + mixed @ w_out[l]
        hn = rmsnorm(h, ffn_norm_g[l])
        h = h + hierarchical_moe(hn, w_group_router[l], b_group_router[l], w_expert_router[l],
                                 b_expert_router[l], w_gate[l], w_up[l], w_down[l])
    return rmsnorm(h, final_norm_g)
```

```python
import functools

import jax
import jax.numpy as jnp
from jax import lax
from jax.experimental import pallas as pl
from jax.experimental.pallas import tpu as pltpu

F32 = jnp.float32
BF16 = jnp.bfloat16
I32 = jnp.int32
U32 = jnp.uint32

D_MODEL = 1024
HEAD_DIM = 64
ATTN_WIDTH = 512
N_Q_HEADS = 8
N_KV_HEADS = 2
Q_PER_KV = 4
KV_WIDTH = 128
BLK = 128
GMLP_WIDTH = 512
IN_PROJ = 1792
N_GROUPS = 4
E_PER_GROUP = 8
N_EXPERTS = 32
D_EXPERT = 256
RMS_EPS = 1e-6
LN_EPS = 1e-5
NEG_INF = -1e30

LANES = 128
HALF = D_MODEL // 2
TM_MIX = 512
TB_ROUTE = 512
TE = 256
TB_COMB = 512
VMEM_LIMIT = 56 * 1024 * 1024

Q_OFF, K_OFF, V_OFF = 0, 512, 640
GU_OFF, GV_OFF = 768, 1280
GROUP_LANE0, EXPERT_LANE0 = 0, 4


def _rms(x, g):
    return (x * lax.rsqrt(jnp.mean(x * x, axis=-1, keepdims=True) + RMS_EPS)) * g


def _gelu(x):
    return 0.5 * x * (1.0 + lax.erf(x * F32(0.7071067811865476)))


def _pack_rows(x):
    lo = pltpu.bitcast(x[:, :HALF].astype(BF16).astype(F32), U32) >> 16
    hi = pltpu.bitcast(x[:, HALF:].astype(BF16).astype(F32), U32) & U32(0xFFFF0000)
    return hi | lo


def _unpack_rows(p):
    lo = pltpu.bitcast(p << 16, F32)
    hi = pltpu.bitcast(p & U32(0xFFFF0000), F32)
    return jnp.concatenate([lo, hi], axis=1)


def _mix_kernel(sinks_ref, x_ref, g1_ref, win_ref, wsp_ref, bsp_ref, lng_ref, lnb_ref,
                ag_ref, gg_ref, wout_ref, h_ref, proj_s, mixed_s, k4p_s, v4p_s):
    j = pl.program_id(1)
    tm = x_ref.shape[0]

    @pl.when(j == 0)
    def _():
        k4p_s[...] = jnp.zeros_like(k4p_s)
        v4p_s[...] = jnp.zeros_like(v4p_s)

    xn = _rms(x_ref[...], g1_ref[...])
    proj_s[...] = jnp.dot(xn.astype(BF16), win_ref[...], preferred_element_type=F32)

    lane128 = lax.broadcasted_iota(I32, (BLK, LANES), 1)
    lo_half = lane128 < HEAD_DIM
    lane256 = lax.broadcasted_iota(I32, (BLK, 2 * LANES), 1)
    head_masks = [(lane256 >= h * HEAD_DIM) & (lane256 < (h + 1) * HEAD_DIM) for h in range(Q_PER_KV)]
    row = lax.broadcasted_iota(I32, (Q_PER_KV * BLK, 2 * BLK), 0) & (BLK - 1)
    col = lax.broadcasted_iota(I32, (Q_PER_KV * BLK, 2 * BLK), 1)
    in_window = (col > row) & (col <= row + BLK)
    first_ok = in_window & ((col >= BLK) | (j > 0))

    trow = lax.broadcasted_iota(I32, (BLK, 2 * BLK), 0)
    tcol = lax.broadcasted_iota(I32, (BLK, 2 * BLK), 1) & (BLK - 1)
    causal2 = tcol <= trow
    wsp = [jnp.where(causal2, wsp_ref[p], 0.0).astype(BF16) for p in range(GMLP_WIDTH // LANES)]

    scale = F32(HEAD_DIM ** -0.5)
    k4_prev = [k4p_s[kvh] for kvh in range(N_KV_HEADS)]
    v4_prev = [v4p_s[kvh] for kvh in range(N_KV_HEADS)]
    for i in range(tm // BLK):
        r0 = i * BLK
        kk = proj_s[r0:r0 + BLK, K_OFF:K_OFF + KV_WIDTH]
        vv = proj_s[r0:r0 + BLK, V_OFF:V_OFF + KV_WIDTH]
        kk_r = pltpu.roll(kk, HEAD_DIM, 1)
        vv_r = pltpu.roll(vv, HEAD_DIM, 1)
        o_parts = []
        for kvh in range(N_KV_HEADS):
            k2 = jnp.where(lo_half, kk, kk_r) if kvh == 0 else jnp.where(lo_half, kk_r, kk)
            v2 = jnp.where(lo_half, vv, vv_r) if kvh == 0 else jnp.where(lo_half, vv_r, vv)
            k4 = jnp.concatenate([k2, k2], axis=1).astype(BF16)
            v4 = jnp.concatenate([v2, v2], axis=1).astype(BF16)
            kband = jnp.concatenate([k4_prev[kvh], k4], axis=0)
            vband = jnp.concatenate([v4_prev[kvh], v4], axis=0)
            k4_prev[kvh] = k4
            v4_prev[kvh] = v4
            q4 = proj_s[r0:r0 + BLK, Q_OFF + kvh * 256:Q_OFF + (kvh + 1) * 256] * scale
            qs = jnp.concatenate([jnp.where(m, q4, 0.0) for m in head_masks], axis=0).astype(BF16)
            s = lax.dot_general(qs, kband, (((1,), (1,)), ((), ())), preferred_element_type=F32)
            s = jnp.where(first_ok if i == 0 else in_window, s, NEG_INF)
            sink = jnp.concatenate(
                [jnp.full((BLK, 1), sinks_ref[kvh * Q_PER_KV + h], F32) for h in range(Q_PER_KV)], axis=0)
            m = jnp.maximum(jnp.max(s, axis=-1, keepdims=True), sink)
            p = jnp.exp(s - m)
            denom = jnp.sum(p, axis=-1, keepdims=True) + jnp.exp(sink - m)
            probs = (p * (1.0 / denom)).astype(BF16)
            pv = jnp.dot(probs, vband, preferred_element_type=F32)
            o = jnp.where(head_masks[0], pv[0:BLK], 0.0)
            for h in range(1, Q_PER_KV):
                o = o + jnp.where(head_masks[h], pv[h * BLK:(h + 1) * BLK], 0.0)
            o_parts.append(o)
        a = jnp.concatenate(o_parts, axis=1)
        mixed_s[r0:r0 + BLK, 0:ATTN_WIDTH] = _rms(a, ag_ref[...]).astype(BF16)

        u = _gelu(proj_s[r0:r0 + BLK, GU_OFF:GU_OFF + GMLP_WIDTH])
        vg = _gelu(proj_s[r0:r0 + BLK, GV_OFF:GV_OFF + GMLP_WIDTH])
        mu = jnp.mean(vg, axis=-1, keepdims=True)
        var = jnp.mean(jnp.square(vg - mu), axis=-1, keepdims=True)
        vln = (vg - mu) * lax.rsqrt(var + LN_EPS) * lng_ref[...] + lnb_ref[...]
        s_parts = []
        for pidx in range(GMLP_WIDTH // LANES):
            vcol = vln[:, pidx * LANES:(pidx + 1) * LANES]
            rhs = jnp.concatenate([jnp.where(lo_half, vcol, 0.0), jnp.where(lo_half, 0.0, vcol)],
                                  axis=0).astype(BF16)
            s_parts.append(jnp.dot(wsp[pidx], rhs, preferred_element_type=F32))
        sg = jnp.concatenate(s_parts, axis=1) + bsp_ref[...]
        g = u * sg
        mixed_s[r0:r0 + BLK, ATTN_WIDTH:D_MODEL] = _rms(g, gg_ref[...]).astype(BF16)

    for kvh in range(N_KV_HEADS):
        k4p_s[kvh] = k4_prev[kvh]
        v4p_s[kvh] = v4_prev[kvh]
    h_ref[...] = x_ref[...] + jnp.dot(mixed_s[...], wout_ref[...], preferred_element_type=F32)


def _mix(x2, sinks, g1, w_in, wsp, bsp, lng, lnb, ag, gg, w_out, batch, seq):
    t = x2.shape[0]
    tm = TM_MIX
    nj = seq // tm
    full = lambda shape: pl.BlockSpec(shape, lambda b, j: (0,) * len(shape))
    return pl.pallas_call(
        _mix_kernel,
        grid=(batch, nj),
        in_specs=[
            pl.BlockSpec(memory_space=pltpu.SMEM),
            pl.BlockSpec((tm, D_MODEL), lambda b, j: (b * nj + j, 0)),
            full((1, D_MODEL)),
            full((D_MODEL, IN_PROJ)),
            full((GMLP_WIDTH // LANES, BLK, 2 * BLK)),
            full((BLK, GMLP_WIDTH)),
            full((1, GMLP_WIDTH)),
            full((1, GMLP_WIDTH)),
            full((1, ATTN_WIDTH)),
            full((1, GMLP_WIDTH)),
            full((D_MODEL, D_MODEL)),
        ],
        out_specs=pl.BlockSpec((tm, D_MODEL), lambda b, j: (b * nj + j, 0)),
        out_shape=jax.ShapeDtypeStruct((t, D_MODEL), F32),
        scratch_shapes=[
            pltpu.VMEM((tm, IN_PROJ), F32),
            pltpu.VMEM((tm, D_MODEL), BF16),
            pltpu.VMEM((N_KV_HEADS, BLK, 2 * LANES), BF16),
            pltpu.VMEM((N_KV_HEADS, BLK, 2 * LANES), BF16),
        ],
        compiler_params=pltpu.CompilerParams(
            dimension_semantics=("arbitrary", "arbitrary"), vmem_limit_bytes=VMEM_LIMIT),
        name="mix",
    )(sinks, x2, g1, w_in, wsp, bsp, lng, lnb, ag, gg, w_out)


def _route_kernel(h_ref, g_ref, whi_ref, wlo_ref, b_ref, xs_hbm, wts_ref, dst_ref, cnt_ref,
                  carry_s, packed_s, dstv_s, dsts_s, sem, *, n_tokens):
    i = pl.program_id(0)
    tb = h_ref.shape[0]

    @pl.when(i == 0)
    def _():
        carry_s[...] = jnp.zeros_like(carry_s)

    hn = _rms(h_ref[...], g_ref[...])
    packed_s[...] = _pack_rows(hn)

    hi = hn.astype(BF16)
    lo = (hn - hi.astype(F32)).astype(BF16)
    logits = (jnp.dot(hi, whi_ref[...], preferred_element_type=F32)
              + (jnp.dot(hi, wlo_ref[...], preferred_element_type=F32)
                 + jnp.dot(lo, whi_ref[...], preferred_element_type=F32))
              + b_ref[...])

    lane = lax.broadcasted_iota(I32, (tb, LANES), 1)
    ninf = F32(-jnp.inf)
    is_group = lane < EXPERT_LANE0
    gl = jnp.where(is_group, logits, ninf)
    gmax = jnp.max(gl, axis=-1, keepdims=True)
    gidx = jnp.min(jnp.where(gl == gmax, lane, LANES), axis=-1, keepdims=True)
    g_w = 1.0 / jnp.sum(jnp.exp(gl - gmax), axis=-1, keepdims=True)

    rel = lane - EXPERT_LANE0
    in_grp = (rel >= 0) & (rel < N_EXPERTS) & ((rel >> 3) == gidx)
    el = jnp.where(in_grp, logits, ninf)
    v1 = jnp.max(el, axis=-1, keepdims=True)
    i1 = jnp.min(jnp.where(el == v1, lane, LANES), axis=-1, keepdims=True)
    el2 = jnp.where(lane == i1, ninf, el)
    v2 = jnp.max(el2, axis=-1, keepdims=True)
    i2 = jnp.min(jnp.where(el2 == v2, lane, LANES), axis=-1, keepdims=True)
    t2 = jnp.exp(v2 - v1)
    inv = 1.0 / (1.0 + t2)
    w1 = inv * g_w
    w2 = (t2 * inv) * g_w

    sel1 = lane == i1
    sel2 = lane == i2
    oh = jnp.where(sel1 | sel2, 1.0, 0.0)
    rr = lax.broadcasted_iota(I32, (tb, tb), 0)
    cc = lax.broadcasted_iota(I32, (tb, tb), 1)
    tril = jnp.where(cc <= rr, 1.0, 0.0).astype(BF16)
    incl = jnp.dot(tril, oh.astype(BF16), preferred_element_type=F32)
    carry = carry_s[...]
    before = incl - oh + carry
    rank1 = jnp.sum(jnp.where(sel1, before, 0.0), axis=-1, keepdims=True)
    rank2 = jnp.sum(jnp.where(sel2, before, 0.0), axis=-1, keepdims=True)
    carry = carry + incl[tb - 1:tb, :]
    carry_s[...] = carry
    cnt_ref[...] = carry

    dst1 = ((i1 - EXPERT_LANE0) * n_tokens).astype(F32) + rank1
    dst2 = ((i2 - EXPERT_LANE0) * n_tokens).astype(F32) + rank2
    wts_ref[...] = jnp.where(lane == 0, w1, jnp.where(lane == 1, w2, 0.0))
    dmat = jnp.where(lane == 0, dst1, jnp.where(lane == 1, dst2, 0.0))
    dst_t = jnp.transpose(dmat)[0:8, :].astype(I32)
    dstv_s[...] = dst_t
    dst_ref[0] = dst_t
    pltpu.sync_copy(dstv_s, dsts_s)

    def issue(t, c):
        pltpu.make_async_copy(packed_s.at[pl.ds(t, 1), :], xs_hbm.at[pl.ds(dsts_s[0, t], 1), :], sem).start()
        pltpu.make_async_copy(packed_s.at[pl.ds(t, 1), :], xs_hbm.at[pl.ds(dsts_s[1, t], 1), :], sem).start()
        return c

    lax.fori_loop(0, tb, issue, 0, unroll=8)

    def drain(t, c):
        pltpu.make_async_copy(packed_s.at[pl.ds(0, 1), :], xs_hbm.at[pl.ds(0, 1), :], sem).wait()
        pltpu.make_async_copy(packed_s.at[pl.ds(0, 1), :], xs_hbm.at[pl.ds(0, 1), :], sem).wait()
        return c

    lax.fori_loop(0, tb, drain, 0, unroll=8)


def _route(h, g, whi, wlo, bias):
    t = h.shape[0]
    tb = TB_ROUTE
    nsteps = t // tb
    full = lambda shape: pl.BlockSpec(shape, lambda i: (0,) * len(shape))
    return pl.pallas_call(
        functools.partial(_route_kernel, n_tokens=t),
        grid=(nsteps,),
        in_specs=[
            pl.BlockSpec((tb, D_MODEL), lambda i: (i, 0)),
            full((1, D_MODEL)),
            full((D_MODEL, LANES)),
            full((D_MODEL, LANES)),
            full((1, LANES)),
        ],
        out_specs=[
            pl.BlockSpec(memory_space=pl.ANY),
            pl.BlockSpec((tb, LANES), lambda i: (i, 0)),
            pl.BlockSpec((1, 8, tb), lambda i: (i, 0, 0)),
            full((1, LANES)),
        ],
        out_shape=[
            jax.ShapeDtypeStruct((N_EXPERTS * t, HALF), U32),
            jax.ShapeDtypeStruct((t, LANES), F32),
            jax.ShapeDtypeStruct((nsteps, 8, tb), I32),
            jax.ShapeDtypeStruct((1, LANES), F32),
        ],
        scratch_shapes=[
            pltpu.VMEM((1, LANES), F32),
            pltpu.VMEM((tb, HALF), U32),
            pltpu.VMEM((8, tb), I32),
            pltpu.SMEM((8, tb), I32),
            pltpu.SemaphoreType.DMA,
        ],
        compiler_params=pltpu.CompilerParams(
            dimension_semantics=("arbitrary",), vmem_limit_bytes=VMEM_LIMIT),
        name="route",
    )(h, g, whi, wlo, bias)


def _pad_kernel(cnt_ref, xs_in, xs_hbm, zeros_s, sem, *, n_tokens):
    del xs_in
    e = pl.program_id(0)
    zeros_s[...] = jnp.zeros_like(zeros_s)
    n = cnt_ref[e]
    pad = (-n) & (TE - 1)
    start = e * n_tokens + n

    def row_copy(r):
        return pltpu.make_async_copy(zeros_s.at[pl.ds(0, 1), :], xs_hbm.at[pl.ds(start + r, 1), :], sem)

    def issue(r, c):
        row_copy(r).start()
        return c

    def drain(r, c):
        row_copy(r).wait()
        return c

    lax.fori_loop(0, pad, issue, 0)
    lax.fori_loop(0, pad, drain, 0)


def _pad(counts, xs, n_tokens):
    return pl.pallas_call(
        functools.partial(_pad_kernel, n_tokens=n_tokens),
        grid_spec=pltpu.PrefetchScalarGridSpec(
            num_scalar_prefetch=1,
            grid=(N_EXPERTS,),
            in_specs=[pl.BlockSpec(memory_space=pl.ANY)],
            out_specs=pl.BlockSpec(memory_space=pl.ANY),
            scratch_shapes=[pltpu.VMEM((8, HALF), U32), pltpu.SemaphoreType.DMA],
        ),
        out_shape=jax.ShapeDtypeStruct(xs.shape, xs.dtype),
        input_output_aliases={1: 0},
        compiler_params=pltpu.CompilerParams(dimension_semantics=("arbitrary",)),
        name="pad",
    )(counts, xs)


def _expert_kernel(te_ref, tb_ref, nt_ref, x_ref, wg_ref, wu_ref, wd_ref, y_ref, wgb_s, wub_s, wdb_s):
    del tb_ref
    i = pl.program_id(0)

    @pl.when(i < nt_ref[0])
    def _():
        prev = te_ref[jnp.maximum(i - 1, 0)]

        @pl.when((i == 0) | (te_ref[i] != prev))
        def _():
            wgb_s[...] = wg_ref[0].astype(BF16)
            wub_s[...] = wu_ref[0].astype(BF16)
            wdb_s[...] = wd_ref[0].astype(BF16)

        x = _unpack_rows(x_ref[...]).astype(BF16)
        a = jnp.dot(x, wgb_s[...], preferred_element_type=F32)
        b = jnp.dot(x, wub_s[...], preferred_element_type=F32)
        hid = (a * jax.nn.sigmoid(a)) * b
        y = jnp.dot(hid.astype(BF16), wdb_s[...], preferred_element_type=F32)
        y_ref[...] = _pack_rows(y)


def _experts(te, tb, nt, xs, w_gate, w_up, w_down, n_tiles):
    return pl.pallas_call(
        _expert_kernel,
        grid_spec=pltpu.PrefetchScalarGridSpec(
            num_scalar_prefetch=3,
            grid=(n_tiles,),
            in_specs=[
                pl.BlockSpec((TE, HALF), lambda i, te, tb, nt: (tb[i], 0)),
                pl.BlockSpec((1, D_MODEL, D_EXPERT), lambda i, te, tb, nt: (te[i], 0, 0)),
                pl.BlockSpec((1, D_MODEL, D_EXPERT), lambda i, te, tb, nt: (te[i], 0, 0)),
                pl.BlockSpec((1, D_EXPERT, D_MODEL), lambda i, te, tb, nt: (te[i], 0, 0)),
            ],
            out_specs=pl.BlockSpec((TE, HALF), lambda i, te, tb, nt: (tb[i], 0)),
            scratch_shapes=[
                pltpu.VMEM((D_MODEL, D_EXPERT), BF16),
                pltpu.VMEM((D_MODEL, D_EXPERT), BF16),
                pltpu.VMEM((D_EXPERT, D_MODEL), BF16),
            ],
        ),
        out_shape=jax.ShapeDtypeStruct(xs.shape, U32),
        compiler_params=pltpu.CompilerParams(
            dimension_semantics=("arbitrary",), vmem_limit_bytes=VMEM_LIMIT),
        name="experts",
    )(te, tb, nt, xs, w_gate, w_up, w_down)


def _combine_kernel(dst_ref, h_ref, wts_ref, g_ref, ys_hbm, out_ref, dsts_s, ybuf_s, sem):
    tb = h_ref.shape[0]
    pltpu.sync_copy(dst_ref.at[0], dsts_s)

    def issue(t, c):
        pltpu.make_async_copy(ys_hbm.at[pl.ds(dsts_s[0, t], 1), :], ybuf_s.at[0, pl.ds(t, 1), :], sem).start()
        pltpu.make_async_copy(ys_hbm.at[pl.ds(dsts_s[1, t], 1), :], ybuf_s.at[1, pl.ds(t, 1), :], sem).start()
        return c

    lax.fori_loop(0, tb, issue, 0, unroll=8)

    def drain(t, c):
        pltpu.make_async_copy(ys_hbm.at[pl.ds(0, 1), :], ybuf_s.at[0, pl.ds(0, 1), :], sem).wait()
        pltpu.make_async_copy(ys_hbm.at[pl.ds(0, 1), :], ybuf_s.at[0, pl.ds(0, 1), :], sem).wait()
        return c

    lax.fori_loop(0, tb, drain, 0, unroll=8)

    w = wts_ref[...]
    moe = w[:, 0:1] * _unpack_rows(ybuf_s[0]) + w[:, 1:2] * _unpack_rows(ybuf_s[1])
    out_ref[...] = _rms(h_ref[...] + moe, g_ref[...])


def _combine(dst, h, wts, g, ys):
    t = h.shape[0]
    tb = TB_COMB
    return pl.pallas_call(
        _combine_kernel,
        grid=(t // tb,),
        in_specs=[
            pl.BlockSpec((1, 8, tb), lambda i: (i, 0, 0)),
            pl.BlockSpec((tb, D_MODEL), lambda i: (i, 0)),
            pl.BlockSpec((tb, LANES), lambda i: (i, 0)),
            pl.BlockSpec((1, D_MODEL), lambda i: (0, 0)),
            pl.BlockSpec(memory_space=pl.ANY),
        ],
        out_specs=pl.BlockSpec((tb, D_MODEL), lambda i: (i, 0)),
        out_shape=jax.ShapeDtypeStruct((t, D_MODEL), F32),
        scratch_shapes=[
            pltpu.SMEM((8, tb), I32),
            pltpu.VMEM((2, tb, HALF), U32),
            pltpu.SemaphoreType.DMA,
        ],
        compiler_params=pltpu.CompilerParams(
            dimension_semantics=("arbitrary",), vmem_limit_bytes=VMEM_LIMIT),
        name="combine",
    )(dst, h, wts, g, ys)


def _tile_tables(counts, n_tokens, n_tiles):
    per = (counts + (TE - 1)) // TE
    cum = jnp.cumsum(per)
    total = cum[-1]
    idx = jnp.minimum(jnp.arange(n_tiles, dtype=I32), total - 1)
    te = jnp.minimum(jnp.searchsorted(cum, idx, side="right").astype(I32), N_EXPERTS - 1)
    local = idx - (cum - per)[te]
    tb = te * (n_tokens // TE) + local
    return te, tb.astype(I32), total.reshape(1).astype(I32)


def kernel(x, mix_norm_g, w_in, attn_sinks, w_spatial, b_spatial, gmlp_ln_g, gmlp_ln_b, attn_out_g, gmlp_out_g, w_out, ffn_norm_g, w_group_router, b_group_router, w_expert_router, b_expert_router, w_gate, w_up, w_down, final_norm_g):
    batch, seq, _ = x.shape
    t = batch * seq
    depth = mix_norm_g.shape[0]
    assert depth == 1, "the final norm is fused into the last layer's combine kernel"
    n_tiles = (2 * t) // TE + N_EXPERTS
    h = x.reshape(t, D_MODEL)
    for l in range(depth):
        n_pairs = GMLP_WIDTH // LANES
        wsp = w_spatial[l].reshape(n_pairs, 2, BLK, BLK).transpose(0, 2, 1, 3).reshape(n_pairs, BLK, 2 * BLK)
        bsp = jnp.repeat(b_spatial[l].T, HEAD_DIM, axis=1)
        h = _mix(h, attn_sinks[l], mix_norm_g[l][None], w_in[l].astype(BF16), wsp, bsp,
                 gmlp_ln_g[l][None], gmlp_ln_b[l][None], attn_out_g[l][None], gmlp_out_g[l][None],
                 w_out[l].astype(BF16), batch, seq)

        w_r = jnp.concatenate(
            [w_group_router[l], w_expert_router[l].transpose(1, 0, 2).reshape(D_MODEL, N_EXPERTS)], axis=1)
        w_r = jnp.pad(w_r, ((0, 0), (0, LANES - w_r.shape[1])))
        b_r = jnp.pad(jnp.concatenate([b_group_router[l], b_expert_router[l].reshape(-1)]),
                      (0, LANES - N_GROUPS - N_EXPERTS))[None]
        w_r_top = lax.bitcast_convert_type(
            lax.bitcast_convert_type(w_r, U32) & U32(0xFFFF0000), F32)
        w_r_hi = w_r_top.astype(BF16)
        w_r_lo = (w_r - w_r_top).astype(BF16)
        xs, wts, dst, cnt = _route(h, ffn_norm_g[l][None], w_r_hi, w_r_lo, b_r)
        counts = cnt[0, EXPERT_LANE0:EXPERT_LANE0 + N_EXPERTS].astype(I32)
        xs = _pad(counts, xs, t)
        te, tb, nt = _tile_tables(counts, t, n_tiles)
        ys = _experts(te, tb, nt, xs, w_gate[l], w_up[l], w_down[l], n_tiles)
        h = _combine(dst, h, wts, final_norm_g[None], ys)
    return h.reshape(batch, seq, D_MODEL)
```

```python
import functools

import jax
import jax.numpy as jnp
from jax import lax
from jax.experimental import pallas as pl
from jax.experimental.pallas import tpu as pltpu

F32 = jnp.float32
BF16 = jnp.bfloat16
I32 = jnp.int32
U32 = jnp.uint32

D_MODEL = 1024
HEAD_DIM = 64
ATTN_WIDTH = 512
N_Q_HEADS = 8
N_KV_HEADS = 2
Q_PER_KV = 4
KV_WIDTH = 128
BLK = 128
GMLP_WIDTH = 512
IN_PROJ = 1792
N_GROUPS = 4
E_PER_GROUP = 8
N_EXPERTS = 32
D_EXPERT = 256
RMS_EPS = 1e-6
LN_EPS = 1e-5
NEG_INF = -1e30

LANES = 128
SUBLANES = 8
CHUNKS = D_MODEL // LANES
TM_MIX = 512
TB_ROUTE = 512
TE = 256
TB_COMB = 512
VMEM_LIMIT = 56 * 1024 * 1024

Q_OFF, K_OFF, V_OFF = 0, 512, 640
GU_OFF, GV_OFF = 768, 1280
GROUP_LANE0, EXPERT_LANE0 = 0, 4


def _rms(x, g):
    return (x * lax.rsqrt(jnp.mean(x * x, axis=-1, keepdims=True) + RMS_EPS)) * g


def _gelu(x):
    return 0.5 * x * (1.0 + lax.erf(x * F32(0.7071067811865476)))


def _store_token_tiles(ref2d, val):
    n = val.shape[0]
    for i in range(n // SUBLANES):
        for c in range(CHUNKS):
            ref2d[pl.ds(i * SUBLANES * CHUNKS + c, SUBLANES, stride=CHUNKS), :] = (
                val[i * SUBLANES:(i + 1) * SUBLANES, c * LANES:(c + 1) * LANES])


def _load_token_tiles(ref2d, n):
    cols = []
    for c in range(CHUNKS):
        cols.append(jnp.concatenate(
            [ref2d[pl.ds(i * SUBLANES * CHUNKS + c, SUBLANES, stride=CHUNKS), :] for i in range(n // SUBLANES)],
            axis=0))
    return jnp.concatenate(cols, axis=1)


def _mix_kernel(sinks_ref, x_ref, g1_ref, win_ref, wsp_ref, bsp_ref, lng_ref, lnb_ref,
                ag_ref, gg_ref, wout_ref, h_ref, proj_s, mixed_s, k4p_s, v4p_s):
    j = pl.program_id(1)
    tm = x_ref.shape[0]

    @pl.when(j == 0)
    def _():
        k4p_s[...] = jnp.zeros_like(k4p_s)
        v4p_s[...] = jnp.zeros_like(v4p_s)

    xn = _rms(x_ref[...], g1_ref[...])
    proj_s[...] = jnp.dot(xn.astype(BF16), win_ref[...], preferred_element_type=F32)

    lane128 = lax.broadcasted_iota(I32, (BLK, LANES), 1)
    lo_half = lane128 < HEAD_DIM
    lane256 = lax.broadcasted_iota(I32, (BLK, 2 * LANES), 1)
    head_masks = [(lane256 >= h * HEAD_DIM) & (lane256 < (h + 1) * HEAD_DIM) for h in range(Q_PER_KV)]
    row = lax.broadcasted_iota(I32, (Q_PER_KV * BLK, 2 * BLK), 0) & (BLK - 1)
    col = lax.broadcasted_iota(I32, (Q_PER_KV * BLK, 2 * BLK), 1)
    in_window = (col > row) & (col <= row + BLK)
    first_ok = in_window & ((col >= BLK) | (j > 0))

    trow = lax.broadcasted_iota(I32, (BLK, 2 * BLK), 0)
    tcol = lax.broadcasted_iota(I32, (BLK, 2 * BLK), 1) & (BLK - 1)
    causal2 = tcol <= trow
    wsp = [jnp.where(causal2, wsp_ref[p], 0.0).astype(BF16) for p in range(GMLP_WIDTH // LANES)]

    scale = F32(HEAD_DIM ** -0.5)
    k4_prev = [k4p_s[kvh] for kvh in range(N_KV_HEADS)]
    v4_prev = [v4p_s[kvh] for kvh in range(N_KV_HEADS)]
    for i in range(tm // BLK):
        r0 = i * BLK
        kk = proj_s[r0:r0 + BLK, K_OFF:K_OFF + KV_WIDTH]
        vv = proj_s[r0:r0 + BLK, V_OFF:V_OFF + KV_WIDTH]
        kk_r = pltpu.roll(kk, HEAD_DIM, 1)
        vv_r = pltpu.roll(vv, HEAD_DIM, 1)
        o_parts = []
        for kvh in range(N_KV_HEADS):
            k2 = jnp.where(lo_half, kk, kk_r) if kvh == 0 else jnp.where(lo_half, kk_r, kk)
            v2 = jnp.where(lo_half, vv, vv_r) if kvh == 0 else jnp.where(lo_half, vv_r, vv)
            k4 = jnp.concatenate([k2, k2], axis=1).astype(BF16)
            v4 = jnp.concatenate([v2, v2], axis=1).astype(BF16)
            kband = jnp.concatenate([k4_prev[kvh], k4], axis=0)
            vband = jnp.concatenate([v4_prev[kvh], v4], axis=0)
            k4_prev[kvh] = k4
            v4_prev[kvh] = v4
            q4 = proj_s[r0:r0 + BLK, Q_OFF + kvh * 256:Q_OFF + (kvh + 1) * 256] * scale
            qs = jnp.concatenate([jnp.where(m, q4, 0.0) for m in head_masks], axis=0).astype(BF16)
            s = lax.dot_general(qs, kband, (((1,), (1,)), ((), ())), preferred_element_type=F32)
            s = jnp.where(first_ok if i == 0 else in_window, s, NEG_INF)
            sink = jnp.concatenate(
                [jnp.full((BLK, 1), sinks_ref[kvh * Q_PER_KV + h], F32) for h in range(Q_PER_KV)], axis=0)
            m = jnp.maximum(jnp.max(s, axis=-1, keepdims=True), sink)
            p = jnp.exp(s - m)
            denom = jnp.sum(p, axis=-1, keepdims=True) + jnp.exp(sink - m)
            probs = (p * (1.0 / denom)).astype(BF16)
            pv = jnp.dot(probs, vband, preferred_element_type=F32)
            o = jnp.where(head_masks[0], pv[0:BLK], 0.0)
            for h in range(1, Q_PER_KV):
                o = o + jnp.where(head_masks[h], pv[h * BLK:(h + 1) * BLK], 0.0)
            o_parts.append(o)
        a = jnp.concatenate(o_parts, axis=1)
        mixed_s[r0:r0 + BLK, 0:ATTN_WIDTH] = _rms(a, ag_ref[...]).astype(BF16)

        u = _gelu(proj_s[r0:r0 + BLK, GU_OFF:GU_OFF + GMLP_WIDTH])
        vg = _gelu(proj_s[r0:r0 + BLK, GV_OFF:GV_OFF + GMLP_WIDTH])
        mu = jnp.mean(vg, axis=-1, keepdims=True)
        var = jnp.mean(jnp.square(vg - mu), axis=-1, keepdims=True)
        vln = (vg - mu) * lax.rsqrt(var + LN_EPS) * lng_ref[...] + lnb_ref[...]
        s_parts = []
        for pidx in range(GMLP_WIDTH // LANES):
            vcol = vln[:, pidx * LANES:(pidx + 1) * LANES]
            rhs = jnp.concatenate([jnp.where(lo_half, vcol, 0.0), jnp.where(lo_half, 0.0, vcol)],
                                  axis=0).astype(BF16)
            s_parts.append(jnp.dot(wsp[pidx], rhs, preferred_element_type=F32))
        sg = jnp.concatenate(s_parts, axis=1) + bsp_ref[...]
        g = u * sg
        mixed_s[r0:r0 + BLK, ATTN_WIDTH:D_MODEL] = _rms(g, gg_ref[...]).astype(BF16)

    for kvh in range(N_KV_HEADS):
        k4p_s[kvh] = k4_prev[kvh]
        v4p_s[kvh] = v4_prev[kvh]
    h_ref[...] = x_ref[...] + jnp.dot(mixed_s[...], wout_ref[...], preferred_element_type=F32)


def _mix(x2, sinks, g1, w_in, wsp, bsp, lng, lnb, ag, gg, w_out, batch, seq):
    t = x2.shape[0]
    tm = TM_MIX
    nj = seq // tm
    full = lambda shape: pl.BlockSpec(shape, lambda b, j: (0,) * len(shape))
    return pl.pallas_call(
        _mix_kernel,
        grid=(batch, nj),
        in_specs=[
            pl.BlockSpec(memory_space=pltpu.SMEM),
            pl.BlockSpec((tm, D_MODEL), lambda b, j: (b * nj + j, 0)),
            full((1, D_MODEL)),
            full((D_MODEL, IN_PROJ)),
            full((GMLP_WIDTH // LANES, BLK, 2 * BLK)),
            full((BLK, GMLP_WIDTH)),
            full((1, GMLP_WIDTH)),
            full((1, GMLP_WIDTH)),
            full((1, ATTN_WIDTH)),
            full((1, GMLP_WIDTH)),
            full((D_MODEL, D_MODEL)),
        ],
        out_specs=pl.BlockSpec((tm, D_MODEL), lambda b, j: (b * nj + j, 0)),
        out_shape=jax.ShapeDtypeStruct((t, D_MODEL), F32),
        scratch_shapes=[
            pltpu.VMEM((tm, IN_PROJ), F32),
            pltpu.VMEM((tm, D_MODEL), BF16),
            pltpu.VMEM((N_KV_HEADS, BLK, 2 * LANES), BF16),
            pltpu.VMEM((N_KV_HEADS, BLK, 2 * LANES), BF16),
        ],
        compiler_params=pltpu.CompilerParams(
            dimension_semantics=("arbitrary", "arbitrary"), vmem_limit_bytes=VMEM_LIMIT),
        name="mix",
    )(sinks, x2, g1, w_in, wsp, bsp, lng, lnb, ag, gg, w_out)


def _route_kernel(h_ref, g_ref, whi_ref, wlo_ref, b_ref, info_ref, cnt_ref, carry_s):
    i = pl.program_id(0)
    tb = h_ref.shape[0]

    @pl.when(i == 0)
    def _():
        carry_s[...] = jnp.zeros_like(carry_s)

    hn = _rms(h_ref[...], g_ref[...])

    hi = hn.astype(BF16)
    lo = (hn - hi.astype(F32)).astype(BF16)
    logits = (jnp.dot(hi, whi_ref[...], preferred_element_type=F32)
              + (jnp.dot(hi, wlo_ref[...], preferred_element_type=F32)
                 + jnp.dot(lo, whi_ref[...], preferred_element_type=F32))
              + b_ref[...])

    lane = lax.broadcasted_iota(I32, (tb, LANES), 1)
    ninf = F32(-jnp.inf)
    is_group = lane < EXPERT_LANE0
    gl = jnp.where(is_group, logits, ninf)
    gmax = jnp.max(gl, axis=-1, keepdims=True)
    gidx = jnp.min(jnp.where(gl == gmax, lane, LANES), axis=-1, keepdims=True)
    g_w = 1.0 / jnp.sum(jnp.exp(gl - gmax), axis=-1, keepdims=True)

    rel = lane - EXPERT_LANE0
    in_grp = (rel >= 0) & (rel < N_EXPERTS) & ((rel >> 3) == gidx)
    el = jnp.where(in_grp, logits, ninf)
    v1 = jnp.max(el, axis=-1, keepdims=True)
    i1 = jnp.min(jnp.where(el == v1, lane, LANES), axis=-1, keepdims=True)
    el2 = jnp.where(lane == i1, ninf, el)
    v2 = jnp.max(el2, axis=-1, keepdims=True)
    i2 = jnp.min(jnp.where(el2 == v2, lane, LANES), axis=-1, keepdims=True)
    t2 = jnp.exp(v2 - v1)
    inv = 1.0 / (1.0 + t2)
    w1 = inv * g_w
    w2 = (t2 * inv) * g_w

    sel1 = lane == i1
    sel2 = lane == i2
    oh = jnp.where(sel1 | sel2, 1.0, 0.0)
    rr = lax.broadcasted_iota(I32, (tb, tb), 0)
    cc = lax.broadcasted_iota(I32, (tb, tb), 1)
    tril = jnp.where(cc <= rr, 1.0, 0.0).astype(BF16)
    incl = jnp.dot(tril, oh.astype(BF16), preferred_element_type=F32)
    carry = carry_s[...]
    before = incl - oh + carry
    rank1 = jnp.sum(jnp.where(sel1, before, 0.0), axis=-1, keepdims=True)
    rank2 = jnp.sum(jnp.where(sel2, before, 0.0), axis=-1, keepdims=True)
    carry = carry + incl[tb - 1:tb, :]
    carry_s[...] = carry
    cnt_ref[...] = carry

    e1 = (i1 - EXPERT_LANE0).astype(F32)
    e2 = (i2 - EXPERT_LANE0).astype(F32)
    info = jnp.zeros((tb, LANES), F32)
    for k, val in enumerate((w1, w2, e1, e2, rank1, rank2)):
        info = jnp.where(lane == k, val, info)
    info_ref[...] = info


def _route(h, g, whi, wlo, bias):
    t = h.shape[0]
    tb = TB_ROUTE
    full = lambda shape: pl.BlockSpec(shape, lambda i: (0,) * len(shape))
    return pl.pallas_call(
        _route_kernel,
        grid=(t // tb,),
        in_specs=[
            pl.BlockSpec((tb, D_MODEL), lambda i: (i, 0)),
            full((1, D_MODEL)),
            full((D_MODEL, LANES)),
            full((D_MODEL, LANES)),
            full((1, LANES)),
        ],
        out_specs=[
            pl.BlockSpec((tb, LANES), lambda i: (i, 0)),
            full((1, LANES)),
        ],
        out_shape=[
            jax.ShapeDtypeStruct((t, LANES), F32),
            jax.ShapeDtypeStruct((1, LANES), F32),
        ],
        scratch_shapes=[pltpu.VMEM((1, LANES), F32)],
        compiler_params=pltpu.CompilerParams(
            dimension_semantics=("arbitrary",), vmem_limit_bytes=VMEM_LIMIT),
        name="route",
    )(h, g, whi, wlo, bias)


INFO_W1, INFO_W2, INFO_E1, INFO_E2, INFO_R1, INFO_R2 = range(6)


def _dispatch_kernel(h_ref, g_ref, info_ref, off_ref, xs_hbm, dst_ref, rows_s, dstv_s, dsts_s, sem):
    i = pl.program_id(0)
    n_steps = pl.num_programs(0)
    tb = h_ref.shape[0]
    slot = i % 2

    def row_copy(buf, t, dst_row):
        return pltpu.make_async_copy(rows_s.at[buf, pl.ds(t * CHUNKS, CHUNKS), :], xs_hbm.at[dst_row], sem)

    def drain_tile():
        for _ in range(2 * tb):
            row_copy(0, 0, 0).wait()

    @pl.when(i >= 2)
    def _():
        drain_tile()

    _store_token_tiles(rows_s.at[slot], _rms(h_ref[...], g_ref[...]))

    info = info_ref[...]
    lane = lax.broadcasted_iota(I32, (tb, LANES), 1).astype(F32)
    off = off_ref[...]
    dsts = []
    for ke, kr in ((INFO_E1, INFO_R1), (INFO_E2, INFO_R2)):
        base = jnp.sum(jnp.where(lane == info[:, ke:ke + 1], off, 0.0), axis=-1, keepdims=True)
        dsts.append(base + info[:, kr:kr + 1])
    dmat = jnp.where(lane == 0.0, dsts[0], jnp.where(lane == 1.0, dsts[1], 0.0))
    dst_t = jnp.transpose(dmat)[0:8, :].astype(I32)
    dstv_s[...] = dst_t
    dst_ref[0] = dst_t
    pltpu.sync_copy(dstv_s, dsts_s)

    for t in range(tb):
        row_copy(slot, t, dsts_s[0, t]).start()
        row_copy(slot, t, dsts_s[1, t]).start()

    @pl.when(i == n_steps - 1)
    def _():
        drain_tile()

        @pl.when(i >= 1)
        def _():
            drain_tile()


def _dispatch(h, g, info, offsets, n_rows):
    t = h.shape[0]
    tb = TB_ROUTE
    nsteps = t // tb
    full = lambda shape: pl.BlockSpec(shape, lambda i: (0,) * len(shape))
    return pl.pallas_call(
        _dispatch_kernel,
        grid=(nsteps,),
        in_specs=[
            pl.BlockSpec((tb, D_MODEL), lambda i: (i, 0)),
            full((1, D_MODEL)),
            pl.BlockSpec((tb, LANES), lambda i: (i, 0)),
            full((1, LANES)),
        ],
        out_specs=[
            pl.BlockSpec(memory_space=pl.ANY),
            pl.BlockSpec((1, 8, tb), lambda i: (i, 0, 0)),
        ],
        out_shape=[
            jax.ShapeDtypeStruct((n_rows, CHUNKS, LANES), F32),
            jax.ShapeDtypeStruct((nsteps, 8, tb), I32),
        ],
        scratch_shapes=[
            pltpu.VMEM((2, tb * CHUNKS, LANES), F32),
            pltpu.VMEM((8, tb), I32),
            pltpu.SMEM((8, tb), I32),
            pltpu.SemaphoreType.DMA,
        ],
        compiler_params=pltpu.CompilerParams(
            dimension_semantics=("arbitrary",), vmem_limit_bytes=VMEM_LIMIT),
        name="dispatch",
    )(h, g, info, offsets)


def _pad_kernel(cnt_ref, off_ref, xs_in, xs_hbm, zeros_s, sem):
    del xs_in
    e = pl.program_id(0)
    zeros_s[...] = jnp.zeros_like(zeros_s)
    n = cnt_ref[e]
    pad = (-n) & (TE - 1)
    pos = off_ref[e] + n
    size = TE // 2
    while size >= 1:
        @pl.when((pad & size) != 0)
        def _(pos=pos, size=size):
            cp = pltpu.make_async_copy(zeros_s.at[pl.ds(0, size)], xs_hbm.at[pl.ds(pos, size)], sem)
            cp.start()
            cp.wait()
        pos = pos + (pad & size)
        size //= 2


def _pad(counts, offsets, xs):
    return pl.pallas_call(
        _pad_kernel,
        grid_spec=pltpu.PrefetchScalarGridSpec(
            num_scalar_prefetch=2,
            grid=(N_EXPERTS,),
            in_specs=[pl.BlockSpec(memory_space=pl.ANY)],
            out_specs=pl.BlockSpec(memory_space=pl.ANY),
            scratch_shapes=[pltpu.VMEM((TE // 2, CHUNKS, LANES), F32), pltpu.SemaphoreType.DMA],
        ),
        out_shape=jax.ShapeDtypeStruct(xs.shape, xs.dtype),
        input_output_aliases={2: 0},
        compiler_params=pltpu.CompilerParams(dimension_semantics=("arbitrary",)),
        name="pad",
    )(counts, offsets, xs)


def _expert_kernel(te_ref, nt_ref, x_ref, wg_ref, wu_ref, wd_ref, y_ref, wgb_s, wub_s, wdb_s):
    i = pl.program_id(0)

    @pl.when(i < nt_ref[0])
    def _():
        prev = te_ref[jnp.maximum(i - 1, 0)]

        @pl.when((i == 0) | (te_ref[i] != prev))
        def _():
            wgb_s[...] = wg_ref[0].astype(BF16)
            wub_s[...] = wu_ref[0].astype(BF16)
            wdb_s[...] = wd_ref[0].astype(BF16)

        x = _load_token_tiles(x_ref, TE).astype(BF16)
        a = jnp.dot(x, wgb_s[...], preferred_element_type=F32)
        b = jnp.dot(x, wub_s[...], preferred_element_type=F32)
        hid = (a * jax.nn.sigmoid(a)) * b
        y = jnp.dot(hid.astype(BF16), wdb_s[...], preferred_element_type=F32)
        _store_token_tiles(y_ref, y)


def _experts(te, nt, xs, w_gate, w_up, w_down, n_tiles):
    tile = lambda i, te, nt: (jnp.minimum(i, nt[0] - 1), 0)
    wblk = lambda i, te, nt: (te[i], 0, 0)
    return pl.pallas_call(
        _expert_kernel,
        grid_spec=pltpu.PrefetchScalarGridSpec(
            num_scalar_prefetch=2,
            grid=(n_tiles,),
            in_specs=[
                pl.BlockSpec((TE * CHUNKS, LANES), tile),
                pl.BlockSpec((1, D_MODEL, D_EXPERT), wblk),
                pl.BlockSpec((1, D_MODEL, D_EXPERT), wblk),
                pl.BlockSpec((1, D_EXPERT, D_MODEL), wblk),
            ],
            out_specs=pl.BlockSpec((TE * CHUNKS, LANES), tile),
            scratch_shapes=[
                pltpu.VMEM((D_MODEL, D_EXPERT), BF16),
                pltpu.VMEM((D_MODEL, D_EXPERT), BF16),
                pltpu.VMEM((D_EXPERT, D_MODEL), BF16),
            ],
        ),
        out_shape=jax.ShapeDtypeStruct(xs.shape, F32),
        compiler_params=pltpu.CompilerParams(
            dimension_semantics=("arbitrary",), vmem_limit_bytes=VMEM_LIMIT),
        name="experts",
    )(te, nt, xs, w_gate, w_up, w_down)


def _combine_kernel(dst0_ref, dstn_ref, h_ref, wts_ref, g_ref, ys_hbm, out_ref, dsts_s, ybuf_s, sem):
    i = pl.program_id(0)
    n_steps = pl.num_programs(0)
    tb = h_ref.shape[0]
    slot = i % 2

    def row_copy(buf, k, t, src_row):
        return pltpu.make_async_copy(ys_hbm.at[src_row], ybuf_s.at[buf, k, pl.ds(t * CHUNKS, CHUNKS), :], sem)

    def issue_tile(buf):
        for t in range(tb):
            row_copy(buf, 0, t, dsts_s[0, t]).start()
            row_copy(buf, 1, t, dsts_s[1, t]).start()

    def drain_tile():
        for _ in range(2 * tb):
            row_copy(0, 0, 0, 0).wait()

    @pl.when(i == 0)
    def _():
        pltpu.sync_copy(dst0_ref.at[0], dsts_s)
        issue_tile(0)

    pltpu.sync_copy(dstn_ref.at[0], dsts_s)
    drain_tile()
    issue_tile(1 - slot)

    w = wts_ref[...]
    moe = (w[:, 0:1] * _load_token_tiles(ybuf_s.at[slot, 0], tb)
           + w[:, 1:2] * _load_token_tiles(ybuf_s.at[slot, 1], tb))
    out_ref[...] = _rms(h_ref[...] + moe, g_ref[...])

    @pl.when(i == n_steps - 1)
    def _():
        drain_tile()


def _combine(dst, h, wts, g, ys):
    t = h.shape[0]
    tb = TB_COMB
    n_steps = t // tb
    return pl.pallas_call(
        _combine_kernel,
        grid=(n_steps,),
        in_specs=[
            pl.BlockSpec((1, 8, tb), lambda i: (0, 0, 0)),
            pl.BlockSpec((1, 8, tb), lambda i: (jnp.minimum(i + 1, n_steps - 1), 0, 0)),
            pl.BlockSpec((tb, D_MODEL), lambda i: (i, 0)),
            pl.BlockSpec((tb, LANES), lambda i: (i, 0)),
            pl.BlockSpec((1, D_MODEL), lambda i: (0, 0)),
            pl.BlockSpec(memory_space=pl.ANY),
        ],
        out_specs=pl.BlockSpec((tb, D_MODEL), lambda i: (i, 0)),
        out_shape=jax.ShapeDtypeStruct((t, D_MODEL), F32),
        scratch_shapes=[
            pltpu.SMEM((8, tb), I32),
            pltpu.VMEM((2, 2, tb * CHUNKS, LANES), F32),
            pltpu.SemaphoreType.DMA,
        ],
        compiler_params=pltpu.CompilerParams(
            dimension_semantics=("arbitrary",), vmem_limit_bytes=VMEM_LIMIT),
        name="combine",
    )(dst, dst, h, wts, g, ys)


def _tile_tables(counts, n_tiles):
    per = (counts + (TE - 1)) // TE
    cum = jnp.cumsum(per)
    total = cum[-1]
    offsets = ((cum - per) * TE).astype(I32)
    idx = jnp.minimum(jnp.arange(n_tiles, dtype=I32), total - 1)
    te = jnp.minimum(jnp.sum((idx[:, None] >= cum[None, :]).astype(I32), axis=1), N_EXPERTS - 1)
    return offsets, te, total.reshape(1).astype(I32)


def kernel(x, mix_norm_g, w_in, attn_sinks, w_spatial, b_spatial, gmlp_ln_g, gmlp_ln_b, attn_out_g, gmlp_out_g, w_out, ffn_norm_g, w_group_router, b_group_router, w_expert_router, b_expert_router, w_gate, w_up, w_down, final_norm_g):
    batch, seq, _ = x.shape
    t = batch * seq
    depth = mix_norm_g.shape[0]
    assert depth == 1, "the final norm is fused into the last layer's combine kernel"
    n_tiles = (2 * t) // TE + N_EXPERTS
    h = x.reshape(t, D_MODEL)
    for l in range(depth):
        n_pairs = GMLP_WIDTH // LANES
        wsp = w_spatial[l].reshape(n_pairs, 2, BLK, BLK).transpose(0, 2, 1, 3).reshape(n_pairs, BLK, 2 * BLK)
        bsp = jnp.repeat(b_spatial[l].T, HEAD_DIM, axis=1)
        h = _mix(h, attn_sinks[l], mix_norm_g[l][None], w_in[l].astype(BF16), wsp, bsp,
                 gmlp_ln_g[l][None], gmlp_ln_b[l][None], attn_out_g[l][None], gmlp_out_g[l][None],
                 w_out[l].astype(BF16), batch, seq)

        w_r = jnp.concatenate(
            [w_group_router[l], w_expert_router[l].transpose(1, 0, 2).reshape(D_MODEL, N_EXPERTS)], axis=1)
        w_r = jnp.pad(w_r, ((0, 0), (0, LANES - w_r.shape[1])))
        b_r = jnp.pad(jnp.concatenate([b_group_router[l], b_expert_router[l].reshape(-1)]),
                      (0, LANES - N_GROUPS - N_EXPERTS))[None]
        w_r_top = lax.bitcast_convert_type(
            lax.bitcast_convert_type(w_r, U32) & U32(0xFFFF0000), F32)
        w_r_hi = w_r_top.astype(BF16)
        w_r_lo = (w_r - w_r_top).astype(BF16)
        info, cnt = _route(h, ffn_norm_g[l][None], w_r_hi, w_r_lo, b_r)
        counts = cnt[0, EXPERT_LANE0:EXPERT_LANE0 + N_EXPERTS].astype(I32)
        offsets, te, nt = _tile_tables(counts, n_tiles)
        off_row = jnp.pad(offsets.astype(F32), (0, LANES - N_EXPERTS))[None]
        xs, dst = _dispatch(h, ffn_norm_g[l][None], info, off_row, n_tiles * TE)
        xs = _pad(counts, offsets, xs)
        ys = _experts(te, nt, xs.reshape(-1, LANES), w_gate[l], w_up[l], w_down[l], n_tiles)
        h = _combine(dst, h, info, final_norm_g[None], ys.reshape(xs.shape))
    return h.reshape(batch, seq, D_MODEL)
```

```python
import functools

import jax
import jax.numpy as jnp
from jax import lax
from jax.experimental import pallas as pl
from jax.experimental.pallas import tpu as pltpu

F32 = jnp.float32
BF16 = jnp.bfloat16
I32 = jnp.int32
U32 = jnp.uint32

D_MODEL = 1024
HEAD_DIM = 64
ATTN_WIDTH = 512
N_Q_HEADS = 8
N_KV_HEADS = 2
Q_PER_KV = 4
KV_WIDTH = 128
BLK = 128
GMLP_WIDTH = 512
IN_PROJ = 1792
N_GROUPS = 4
E_PER_GROUP = 8
N_EXPERTS = 32
D_EXPERT = 256
RMS_EPS = 1e-6
LN_EPS = 1e-5
NEG_INF = -1e30

LANES = 128
SUBLANES = 8
CHUNKS = D_MODEL // LANES
TM_MIX = 512
TB_ROUTE = 512
TE = 512
TB_COMB = 512
VMEM_LIMIT = 56 * 1024 * 1024

Q_OFF, K_OFF, V_OFF = 0, 512, 640
GU_OFF, GV_OFF = 768, 1280
GROUP_LANE0, EXPERT_LANE0 = 0, 4


def _rms(x, g):
    return (x * lax.rsqrt(jnp.mean(x * x, axis=-1, keepdims=True) + RMS_EPS)) * g


def _gelu(x):
    return 0.5 * x * (1.0 + lax.erf(x * F32(0.7071067811865476)))


def _store_token_tiles(ref2d, val):
    n = val.shape[0]
    for i in range(n // SUBLANES):
        for c in range(CHUNKS):
            ref2d[pl.ds(i * SUBLANES * CHUNKS + c, SUBLANES, stride=CHUNKS), :] = (
                val[i * SUBLANES:(i + 1) * SUBLANES, c * LANES:(c + 1) * LANES])


def _load_token_tiles(ref2d, n):
    cols = []
    for c in range(CHUNKS):
        cols.append(jnp.concatenate(
            [ref2d[pl.ds(i * SUBLANES * CHUNKS + c, SUBLANES, stride=CHUNKS), :] for i in range(n // SUBLANES)],
            axis=0))
    return jnp.concatenate(cols, axis=1)


def _mix_kernel(sinks_ref, x_ref, g1_ref, win_ref, wsp_ref, bsp_ref, lng_ref, lnb_ref,
                ag_ref, gg_ref, wout_ref, h_ref, proj_s, mixed_s, k4p_s, v4p_s):
    j = pl.program_id(1)
    tm = x_ref.shape[0]

    @pl.when(j == 0)
    def _():
        k4p_s[...] = jnp.zeros_like(k4p_s)
        v4p_s[...] = jnp.zeros_like(v4p_s)

    xn = _rms(x_ref[...], g1_ref[...])
    proj_s[...] = jnp.dot(xn.astype(BF16), win_ref[...], preferred_element_type=F32)

    lane128 = lax.broadcasted_iota(I32, (BLK, LANES), 1)
    lo_half = lane128 < HEAD_DIM
    lane256 = lax.broadcasted_iota(I32, (BLK, 2 * LANES), 1)
    head_masks = [(lane256 >= h * HEAD_DIM) & (lane256 < (h + 1) * HEAD_DIM) for h in range(Q_PER_KV)]
    key = lax.broadcasted_iota(I32, (2 * BLK, Q_PER_KV * BLK), 0)
    qry = lax.broadcasted_iota(I32, (2 * BLK, Q_PER_KV * BLK), 1) & (BLK - 1)
    in_window = (key > qry) & (key <= qry + BLK)
    first_ok = in_window & ((key >= BLK) | (j > 0))
    orow = lax.broadcasted_iota(I32, (2 * LANES, BLK), 0)
    out_masks = [(orow >= h * HEAD_DIM) & (orow < (h + 1) * HEAD_DIM) for h in range(Q_PER_KV)]

    trow = lax.broadcasted_iota(I32, (BLK, 2 * BLK), 0)
    tcol = lax.broadcasted_iota(I32, (BLK, 2 * BLK), 1) & (BLK - 1)
    causal2 = tcol <= trow
    wsp = [jnp.where(causal2, wsp_ref[p], 0.0).astype(BF16) for p in range(GMLP_WIDTH // LANES)]

    scale = F32(HEAD_DIM ** -0.5)
    k4_prev = [k4p_s[kvh] for kvh in range(N_KV_HEADS)]
    v4t_prev = [v4p_s[kvh] for kvh in range(N_KV_HEADS)]
    for i in range(tm // BLK):
        r0 = i * BLK
        kk = proj_s[r0:r0 + BLK, K_OFF:K_OFF + KV_WIDTH]
        vv = proj_s[r0:r0 + BLK, V_OFF:V_OFF + KV_WIDTH]
        kk_r = pltpu.roll(kk, HEAD_DIM, 1)
        vv_r = pltpu.roll(vv, HEAD_DIM, 1)
        o_parts = []
        for kvh in range(N_KV_HEADS):
            k2 = jnp.where(lo_half, kk, kk_r) if kvh == 0 else jnp.where(lo_half, kk_r, kk)
            v2 = jnp.where(lo_half, vv, vv_r) if kvh == 0 else jnp.where(lo_half, vv_r, vv)
            k4 = jnp.concatenate([k2, k2], axis=1).astype(BF16)
            v4t = jnp.transpose(jnp.concatenate([v2, v2], axis=1)).astype(BF16)
            kband = jnp.concatenate([k4_prev[kvh], k4], axis=0)
            vband_t = jnp.concatenate([v4t_prev[kvh], v4t], axis=1)
            k4_prev[kvh] = k4
            v4t_prev[kvh] = v4t
            q4 = proj_s[r0:r0 + BLK, Q_OFF + kvh * 256:Q_OFF + (kvh + 1) * 256] * scale
            qs = jnp.concatenate([jnp.where(m, q4, 0.0) for m in head_masks], axis=0).astype(BF16)
            st = lax.dot_general(kband, qs, (((1,), (1,)), ((), ())),
                                 preferred_element_type=F32)
            st = jnp.where(first_ok if i == 0 else in_window, st, NEG_INF)
            sink = jnp.concatenate(
                [jnp.full((1, BLK), sinks_ref[kvh * Q_PER_KV + h], F32) for h in range(Q_PER_KV)], axis=1)
            m = jnp.maximum(jnp.max(st, axis=0, keepdims=True), sink)
            p = jnp.exp(st - m)
            denom = jnp.sum(p, axis=0, keepdims=True) + jnp.exp(sink - m)
            probs_t = (p * (1.0 / denom)).astype(BF16)
            pvt = jnp.dot(vband_t, probs_t, preferred_element_type=F32)
            zt = jnp.where(out_masks[0], pvt[:, 0:BLK], 0.0)
            for h in range(1, Q_PER_KV):
                zt = zt + jnp.where(out_masks[h], pvt[:, h * BLK:(h + 1) * BLK], 0.0)
            o_parts.append(jnp.transpose(zt))
        a = jnp.concatenate(o_parts, axis=1)
        mixed_s[r0:r0 + BLK, 0:ATTN_WIDTH] = _rms(a, ag_ref[...]).astype(BF16)

        u = _gelu(proj_s[r0:r0 + BLK, GU_OFF:GU_OFF + GMLP_WIDTH])
        vg = _gelu(proj_s[r0:r0 + BLK, GV_OFF:GV_OFF + GMLP_WIDTH])
        mu = jnp.mean(vg, axis=-1, keepdims=True)
        var = jnp.mean(jnp.square(vg - mu), axis=-1, keepdims=True)
        vln = (vg - mu) * lax.rsqrt(var + LN_EPS) * lng_ref[...] + lnb_ref[...]
        s_parts = []
        for pidx in range(GMLP_WIDTH // LANES):
            vcol = vln[:, pidx * LANES:(pidx + 1) * LANES]
            rhs = jnp.concatenate([jnp.where(lo_half, vcol, 0.0), jnp.where(lo_half, 0.0, vcol)],
                                  axis=0).astype(BF16)
            s_parts.append(jnp.dot(wsp[pidx], rhs, preferred_element_type=F32))
        sg = jnp.concatenate(s_parts, axis=1) + bsp_ref[...]
        g = u * sg
        mixed_s[r0:r0 + BLK, ATTN_WIDTH:D_MODEL] = _rms(g, gg_ref[...]).astype(BF16)

    for kvh in range(N_KV_HEADS):
        k4p_s[kvh] = k4_prev[kvh]
        v4p_s[kvh] = v4t_prev[kvh]
    h_ref[...] = x_ref[...] + jnp.dot(mixed_s[...], wout_ref[...], preferred_element_type=F32)


def _mix(x2, sinks, g1, w_in, wsp, bsp, lng, lnb, ag, gg, w_out, batch, seq):
    t = x2.shape[0]
    tm = TM_MIX
    nj = seq // tm
    full = lambda shape: pl.BlockSpec(shape, lambda b, j: (0,) * len(shape))
    return pl.pallas_call(
        _mix_kernel,
        grid=(batch, nj),
        in_specs=[
            pl.BlockSpec(memory_space=pltpu.SMEM),
            pl.BlockSpec((tm, D_MODEL), lambda b, j: (b * nj + j, 0)),
            full((1, D_MODEL)),
            full((D_MODEL, IN_PROJ)),
            full((GMLP_WIDTH // LANES, BLK, 2 * BLK)),
            full((BLK, GMLP_WIDTH)),
            full((1, GMLP_WIDTH)),
            full((1, GMLP_WIDTH)),
            full((1, ATTN_WIDTH)),
            full((1, GMLP_WIDTH)),
            full((D_MODEL, D_MODEL)),
        ],
        out_specs=pl.BlockSpec((tm, D_MODEL), lambda b, j: (b * nj + j, 0)),
        out_shape=jax.ShapeDtypeStruct((t, D_MODEL), F32),
        scratch_shapes=[
            pltpu.VMEM((tm, IN_PROJ), F32),
            pltpu.VMEM((tm, D_MODEL), BF16),
            pltpu.VMEM((N_KV_HEADS, BLK, 2 * LANES), BF16),
            pltpu.VMEM((N_KV_HEADS, 2 * LANES, BLK), BF16),
        ],
        compiler_params=pltpu.CompilerParams(
            dimension_semantics=("arbitrary", "arbitrary"), vmem_limit_bytes=VMEM_LIMIT),
        name="mix",
    )(sinks, x2, g1, w_in, wsp, bsp, lng, lnb, ag, gg, w_out)


def _route_kernel(h_ref, g_ref, whi_ref, wlo_ref, b_ref, info_ref, cnt_ref, carry_s):
    i = pl.program_id(0)
    tb = h_ref.shape[0]

    @pl.when(i == 0)
    def _():
        carry_s[...] = jnp.zeros_like(carry_s)

    hn = _rms(h_ref[...], g_ref[...])

    hi = hn.astype(BF16)
    lo = (hn - hi.astype(F32)).astype(BF16)
    logits = (jnp.dot(hi, whi_ref[...], preferred_element_type=F32)
              + (jnp.dot(hi, wlo_ref[...], preferred_element_type=F32)
                 + jnp.dot(lo, whi_ref[...], preferred_element_type=F32))
              + b_ref[...])

    lane = lax.broadcasted_iota(I32, (tb, LANES), 1)
    ninf = F32(-jnp.inf)
    is_group = lane < EXPERT_LANE0
    gl = jnp.where(is_group, logits, ninf)
    gmax = jnp.max(gl, axis=-1, keepdims=True)
    gidx = jnp.min(jnp.where(gl == gmax, lane, LANES), axis=-1, keepdims=True)
    g_w = 1.0 / jnp.sum(jnp.exp(gl - gmax), axis=-1, keepdims=True)

    rel = lane - EXPERT_LANE0
    in_grp = (rel >= 0) & (rel < N_EXPERTS) & ((rel >> 3) == gidx)
    el = jnp.where(in_grp, logits, ninf)
    v1 = jnp.max(el, axis=-1, keepdims=True)
    i1 = jnp.min(jnp.where(el == v1, lane, LANES), axis=-1, keepdims=True)
    el2 = jnp.where(lane == i1, ninf, el)
    v2 = jnp.max(el2, axis=-1, keepdims=True)
    i2 = jnp.min(jnp.where(el2 == v2, lane, LANES), axis=-1, keepdims=True)
    t2 = jnp.exp(v2 - v1)
    inv = 1.0 / (1.0 + t2)
    w1 = inv * g_w
    w2 = (t2 * inv) * g_w

    sel1 = lane == i1
    sel2 = lane == i2
    oh = jnp.where(sel1 | sel2, 1.0, 0.0)
    rr = lax.broadcasted_iota(I32, (tb, tb), 0)
    cc = lax.broadcasted_iota(I32, (tb, tb), 1)
    tril = jnp.where(cc <= rr, 1.0, 0.0).astype(BF16)
    incl = jnp.dot(tril, oh.astype(BF16), preferred_element_type=F32)
    carry = carry_s[...]
    before = incl - oh + carry
    rank1 = jnp.sum(jnp.where(sel1, before, 0.0), axis=-1, keepdims=True)
    rank2 = jnp.sum(jnp.where(sel2, before, 0.0), axis=-1, keepdims=True)
    carry = carry + incl[tb - 1:tb, :]
    carry_s[...] = carry
    cnt_ref[...] = carry

    e1 = (i1 - EXPERT_LANE0).astype(F32)
    e2 = (i2 - EXPERT_LANE0).astype(F32)
    info = jnp.zeros((tb, LANES), F32)
    for k, val in enumerate((w1, w2, e1, e2, rank1, rank2)):
        info = jnp.where(lane == k, val, info)
    info_ref[...] = info


def _route(h, g, whi, wlo, bias):
    t = h.shape[0]
    tb = TB_ROUTE
    full = lambda shape: pl.BlockSpec(shape, lambda i: (0,) * len(shape))
    return pl.pallas_call(
        _route_kernel,
        grid=(t // tb,),
        in_specs=[
            pl.BlockSpec((tb, D_MODEL), lambda i: (i, 0)),
            full((1, D_MODEL)),
            full((D_MODEL, LANES)),
            full((D_MODEL, LANES)),
            full((1, LANES)),
        ],
        out_specs=[
            pl.BlockSpec((tb, LANES), lambda i: (i, 0)),
            full((1, LANES)),
        ],
        out_shape=[
            jax.ShapeDtypeStruct((t, LANES), F32),
            jax.ShapeDtypeStruct((1, LANES), F32),
        ],
        scratch_shapes=[pltpu.VMEM((1, LANES), F32)],
        compiler_params=pltpu.CompilerParams(
            dimension_semantics=("arbitrary",), vmem_limit_bytes=VMEM_LIMIT),
        name="route",
    )(h, g, whi, wlo, bias)


INFO_W1, INFO_W2, INFO_E1, INFO_E2, INFO_R1, INFO_R2 = range(6)


def _dispatch_kernel(cnt_ref, off_ref, nt_ref, h_ref, g_ref, info_ref, offrow_ref, xs_hbm, dst_ref,
                     rows_s, zeros_s, dstv_s, dsts_s, sems, fill_sem, *, n_tiles):
    i = pl.program_id(0)
    n_steps = pl.num_programs(0)
    tb = h_ref.shape[0]
    slot = i % 2

    def row_copy(buf, t, dst_row):
        return pltpu.make_async_copy(rows_s.at[buf, pl.ds(t * CHUNKS, CHUNKS), :], xs_hbm.at[dst_row],
                                     sems.at[buf])

    def drain_tile(buf):
        for _ in range(2 * tb):
            row_copy(buf, 0, 0).wait()

    @pl.when(i >= 2)
    def _():
        drain_tile(slot)

    _store_token_tiles(rows_s.at[slot], _rms(h_ref[...], g_ref[...]))

    info = info_ref[...]
    lane = lax.broadcasted_iota(I32, (tb, LANES), 1).astype(F32)
    off = offrow_ref[...]
    dsts = []
    for ke, kr in ((INFO_E1, INFO_R1), (INFO_E2, INFO_R2)):
        base = jnp.sum(jnp.where(lane == info[:, ke:ke + 1], off, 0.0), axis=-1, keepdims=True)
        dsts.append(base + info[:, kr:kr + 1])
    dmat = jnp.where(lane == 0.0, dsts[0], jnp.where(lane == 1.0, dsts[1], 0.0))
    dst_t = jnp.transpose(dmat)[0:8, :].astype(I32)
    dstv_s[...] = dst_t
    dst_ref[0] = dst_t
    pltpu.sync_copy(dstv_s, dsts_s)

    for t in range(tb):
        row_copy(slot, t, dsts_s[0, t]).start()
        row_copy(slot, t, dsts_s[1, t]).start()

    @pl.when(i == n_steps - 1)
    def _():
        zeros_s[...] = jnp.zeros_like(zeros_s)
        half = TE // 2

        def fill_copy(pos, size):
            return pltpu.make_async_copy(zeros_s.at[pl.ds(0, size)], xs_hbm.at[pl.ds(pos, size)], fill_sem)

        def for_each_gap(action):
            for e in range(N_EXPERTS):
                n = cnt_ref[e]
                gap = (-n) & (TE - 1)
                pos = off_ref[e] + n
                size = half
                while size >= 1:
                    @pl.when((gap & size) != 0)
                    def _(pos=pos, size=size):
                        action(fill_copy(pos, size))
                    pos = pos + (gap & size)
                    size //= 2

        def tail_issue(k, c):
            fill_copy(k * half, half).start()
            return c

        def tail_wait(k, c):
            fill_copy(0, half).wait()
            return c

        for_each_gap(lambda cp: cp.start())
        lax.fori_loop(2 * nt_ref[0], 2 * n_tiles, tail_issue, 0)
        for_each_gap(lambda cp: cp.wait())
        lax.fori_loop(2 * nt_ref[0], 2 * n_tiles, tail_wait, 0)
        drain_tile(slot)

        @pl.when(i >= 1)
        def _():
            drain_tile(1 - slot)


def _dispatch(counts, offsets, nt, h, g, info, off_row, n_tiles):
    t = h.shape[0]
    tb = TB_ROUTE
    nsteps = t // tb
    full = lambda shape: pl.BlockSpec(shape, lambda i, *_: (0,) * len(shape))
    return pl.pallas_call(
        functools.partial(_dispatch_kernel, n_tiles=n_tiles),
        grid_spec=pltpu.PrefetchScalarGridSpec(
            num_scalar_prefetch=3,
            grid=(nsteps,),
            in_specs=[
                pl.BlockSpec((tb, D_MODEL), lambda i, *_: (i, 0)),
                full((1, D_MODEL)),
                pl.BlockSpec((tb, LANES), lambda i, *_: (i, 0)),
                full((1, LANES)),
            ],
            out_specs=[
                pl.BlockSpec(memory_space=pl.ANY),
                pl.BlockSpec((1, 8, tb), lambda i, *_: (i, 0, 0)),
            ],
            scratch_shapes=[
                pltpu.VMEM((2, tb * CHUNKS, LANES), F32),
                pltpu.VMEM((TE // 2, CHUNKS, LANES), F32),
                pltpu.VMEM((8, tb), I32),
                pltpu.SMEM((8, tb), I32),
                pltpu.SemaphoreType.DMA((2,)),
                pltpu.SemaphoreType.DMA,
            ],
        ),
        out_shape=[
            jax.ShapeDtypeStruct((n_tiles * TE, CHUNKS, LANES), F32),
            jax.ShapeDtypeStruct((nsteps, 8, tb), I32),
        ],
        compiler_params=pltpu.CompilerParams(
            dimension_semantics=("arbitrary",), vmem_limit_bytes=VMEM_LIMIT),
        name="dispatch",
    )(counts, offsets, nt, h, g, info, off_row)


def _expert_kernel(te_ref, nt_ref, x_ref, wg_ref, wu_ref, wd_ref, y_ref, wgu_s, wdb_s):
    i = pl.program_id(0)

    @pl.when(i >= nt_ref[0])
    def _():
        y_ref[...] = jnp.zeros_like(y_ref)

    @pl.when(i < nt_ref[0])
    def _():
        prev = te_ref[jnp.maximum(i - 1, 0)]

        @pl.when((i == 0) | (te_ref[i] != prev))
        def _():
            wgu_s[:, 0:D_EXPERT] = wg_ref[0].astype(BF16)
            wgu_s[:, D_EXPERT:2 * D_EXPERT] = wu_ref[0].astype(BF16)
            wdb_s[...] = wd_ref[0].astype(BF16)

        x = _load_token_tiles(x_ref, TE).astype(BF16)
        ab = jnp.dot(x, wgu_s[...], preferred_element_type=F32)
        a = ab[:, 0:D_EXPERT]
        b = ab[:, D_EXPERT:2 * D_EXPERT]
        hid = (a * jax.nn.sigmoid(a)) * b
        y = jnp.dot(hid.astype(BF16), wdb_s[...], preferred_element_type=F32)
        _store_token_tiles(y_ref, y)


def _experts(te, nt, xs, w_gate, w_up, w_down, n_tiles):
    tile = lambda i, te, nt: (i, 0)
    wblk = lambda i, te, nt: (te[i], 0, 0)
    return pl.pallas_call(
        _expert_kernel,
        grid_spec=pltpu.PrefetchScalarGridSpec(
            num_scalar_prefetch=2,
            grid=(n_tiles,),
            in_specs=[
                pl.BlockSpec((TE * CHUNKS, LANES), tile),
                pl.BlockSpec((1, D_MODEL, D_EXPERT), wblk),
                pl.BlockSpec((1, D_MODEL, D_EXPERT), wblk),
                pl.BlockSpec((1, D_EXPERT, D_MODEL), wblk),
            ],
            out_specs=pl.BlockSpec((TE * CHUNKS, LANES), tile),
            scratch_shapes=[
                pltpu.VMEM((D_MODEL, 2 * D_EXPERT), BF16),
                pltpu.VMEM((D_EXPERT, D_MODEL), BF16),
            ],
        ),
        out_shape=jax.ShapeDtypeStruct(xs.shape, F32),
        compiler_params=pltpu.CompilerParams(
            dimension_semantics=("arbitrary",), vmem_limit_bytes=VMEM_LIMIT),
        name="experts",
    )(te, nt, xs, w_gate, w_up, w_down)


def _combine_kernel(dst0_ref, dstn_ref, h_ref, wts_ref, g_ref, ys_hbm, out_ref, dsts_s, ybuf_s, sem):
    i = pl.program_id(0)
    n_steps = pl.num_programs(0)
    tb = h_ref.shape[0]
    slot = i % 2

    def row_copy(buf, k, t, src_row):
        return pltpu.make_async_copy(ys_hbm.at[src_row], ybuf_s.at[buf, k, pl.ds(t * CHUNKS, CHUNKS), :], sem)

    def issue_tile(buf):
        for t in range(tb):
            row_copy(buf, 0, t, dsts_s[0, t]).start()
            row_copy(buf, 1, t, dsts_s[1, t]).start()

    def drain_tile():
        for _ in range(2 * tb):
            row_copy(0, 0, 0, 0).wait()

    @pl.when(i == 0)
    def _():
        pltpu.sync_copy(dst0_ref.at[0], dsts_s)
        issue_tile(0)

    pltpu.sync_copy(dstn_ref.at[0], dsts_s)
    drain_tile()
    issue_tile(1 - slot)

    w = wts_ref[...]
    moe = (w[:, 0:1] * _load_token_tiles(ybuf_s.at[slot, 0], tb)
           + w[:, 1:2] * _load_token_tiles(ybuf_s.at[slot, 1], tb))
    out_ref[...] = _rms(h_ref[...] + moe, g_ref[...])

    @pl.when(i == n_steps - 1)
    def _():
        drain_tile()


def _combine(dst, h, wts, g, ys):
    t = h.shape[0]
    tb = TB_COMB
    n_steps = t // tb
    return pl.pallas_call(
        _combine_kernel,
        grid=(n_steps,),
        in_specs=[
            pl.BlockSpec((1, 8, tb), lambda i: (0, 0, 0)),
            pl.BlockSpec((1, 8, tb), lambda i: (jnp.minimum(i + 1, n_steps - 1), 0, 0)),
            pl.BlockSpec((tb, D_MODEL), lambda i: (i, 0)),
            pl.BlockSpec((tb, LANES), lambda i: (i, 0)),
            pl.BlockSpec((1, D_MODEL), lambda i: (0, 0)),
            pl.BlockSpec(memory_space=pl.ANY),
        ],
        out_specs=pl.BlockSpec((tb, D_MODEL), lambda i: (i, 0)),
        out_shape=jax.ShapeDtypeStruct((t, D_MODEL), F32),
        scratch_shapes=[
            pltpu.SMEM((8, tb), I32),
            pltpu.VMEM((2, 2, tb * CHUNKS, LANES), F32),
            pltpu.SemaphoreType.DMA,
        ],
        compiler_params=pltpu.CompilerParams(
            dimension_semantics=("arbitrary",), vmem_limit_bytes=VMEM_LIMIT),
        name="combine",
    )(dst, dst, h, wts, g, ys)


def _tile_tables(counts, n_tiles):
    per = (counts + (TE - 1)) // TE
    cum = jnp.cumsum(per)
    total = cum[-1]
    offsets = ((cum - per) * TE).astype(I32)
    idx = jnp.minimum(jnp.arange(n_tiles, dtype=I32), total - 1)
    te = jnp.minimum(jnp.sum((idx[:, None] >= cum[None, :]).astype(I32), axis=1), N_EXPERTS - 1)
    return offsets, te, total.reshape(1).astype(I32)


def kernel(x, mix_norm_g, w_in, attn_sinks, w_spatial, b_spatial, gmlp_ln_g, gmlp_ln_b, attn_out_g, gmlp_out_g, w_out, ffn_norm_g, w_group_router, b_group_router, w_expert_router, b_expert_router, w_gate, w_up, w_down, final_norm_g):
    batch, seq, _ = x.shape
    t = batch * seq
    depth = mix_norm_g.shape[0]
    assert depth == 1, "the final norm is fused into the last layer's combine kernel"
    n_tiles = (2 * t) // TE + N_EXPERTS
    h = x.reshape(t, D_MODEL)
    for l in range(depth):
        n_pairs = GMLP_WIDTH // LANES
        wsp = w_spatial[l].reshape(n_pairs, 2, BLK, BLK).transpose(0, 2, 1, 3).reshape(n_pairs, BLK, 2 * BLK)
        bsp = jnp.repeat(b_spatial[l].T, HEAD_DIM, axis=1)
        h = _mix(h, attn_sinks[l], mix_norm_g[l][None], w_in[l].astype(BF16), wsp, bsp,
                 gmlp_ln_g[l][None], gmlp_ln_b[l][None], attn_out_g[l][None], gmlp_out_g[l][None],
                 w_out[l].astype(BF16), batch, seq)

        w_r = jnp.concatenate(
            [w_group_router[l], w_expert_router[l].transpose(1, 0, 2).reshape(D_MODEL, N_EXPERTS)], axis=1)
        w_r = jnp.pad(w_r, ((0, 0), (0, LANES - w_r.shape[1])))
        b_r = jnp.pad(jnp.concatenate([b_group_router[l], b_expert_router[l].reshape(-1)]),
                      (0, LANES - N_GROUPS - N_EXPERTS))[None]
        w_r_top = lax.bitcast_convert_type(
            lax.bitcast_convert_type(w_r, U32) & U32(0xFFFF0000), F32)
        w_r_hi = w_r_top.astype(BF16)
        w_r_lo = (w_r - w_r_top).astype(BF16)
        info, cnt = _route(h, ffn_norm_g[l][None], w_r_hi, w_r_lo, b_r)
        counts = cnt[0, EXPERT_LANE0:EXPERT_LANE0 + N_EXPERTS].astype(I32)
        offsets, te, nt = _tile_tables(counts, n_tiles)
        off_row = jnp.pad(offsets.astype(F32), (0, LANES - N_EXPERTS))[None]
        xs, dst = _dispatch(counts, offsets, nt, h, ffn_norm_g[l][None], info, off_row, n_tiles)
        ys = _experts(te, nt, xs.reshape(-1, LANES), w_gate[l], w_up[l], w_down[l], n_tiles)
        h = _combine(dst, h, info, final_norm_g[None], ys.reshape(xs.shape))
    return h.reshape(batch, seq, D_MODEL)
```

```python
import functools

import jax
import jax.numpy as jnp
from jax import lax
from jax.experimental import pallas as pl
from jax.experimental.pallas import tpu as pltpu

F32 = jnp.float32
BF16 = jnp.bfloat16
I32 = jnp.int32
U32 = jnp.uint32

D_MODEL = 1024
HEAD_DIM = 64
ATTN_WIDTH = 512
N_Q_HEADS = 8
N_KV_HEADS = 2
Q_PER_KV = 4
KV_WIDTH = 128
BLK = 128
GMLP_WIDTH = 512
IN_PROJ = 1792
N_GROUPS = 4
E_PER_GROUP = 8
N_EXPERTS = 32
D_EXPERT = 256
RMS_EPS = 1e-6
LN_EPS = 1e-5
NEG_INF = -1e30

LANES = 128
SUBLANES = 8
CHUNKS = D_MODEL // LANES
TM_MIX = 512
TB_ROUTE = 512
TE = 512
TB_COMB = 512
VMEM_LIMIT = 56 * 1024 * 1024

Q_OFF, K_OFF, V_OFF = 0, 512, 640
GU_OFF, GV_OFF = 768, 1280
GROUP_ROW0, EXPERT_ROW0 = 0, 8


def _rms(x, g):
    return (x * lax.rsqrt(jnp.mean(x * x, axis=-1, keepdims=True) + RMS_EPS)) * g


def _gelu(x):
    return 0.5 * x * (1.0 + lax.erf(x * F32(0.7071067811865476)))


def _store_token_tiles(ref2d, val):
    n = val.shape[0]
    for i in range(n // SUBLANES):
        for c in range(CHUNKS):
            ref2d[pl.ds(i * SUBLANES * CHUNKS + c, SUBLANES, stride=CHUNKS), :] = (
                val[i * SUBLANES:(i + 1) * SUBLANES, c * LANES:(c + 1) * LANES])


def _load_token_tiles(ref2d, n):
    cols = []
    for c in range(CHUNKS):
        cols.append(jnp.concatenate(
            [ref2d[pl.ds(i * SUBLANES * CHUNKS + c, SUBLANES, stride=CHUNKS), :] for i in range(n // SUBLANES)],
            axis=0))
    return jnp.concatenate(cols, axis=1)


def _mix_kernel(sinks_ref, x_ref, g1_ref, win_ref, wsp_ref, bsp_ref, lng_ref, lnb_ref,
                ag_ref, gg_ref, wout_ref, h_ref, proj_s, mixed_s, k4p_s, v4p_s):
    j = pl.program_id(1)
    tm = x_ref.shape[0]

    @pl.when(j == 0)
    def _():
        k4p_s[...] = jnp.zeros_like(k4p_s)
        v4p_s[...] = jnp.zeros_like(v4p_s)

    xn = _rms(x_ref[...], g1_ref[...])
    proj_s[...] = jnp.dot(xn.astype(BF16), win_ref[...], preferred_element_type=F32)

    lane128 = lax.broadcasted_iota(I32, (BLK, LANES), 1)
    lo_half = lane128 < HEAD_DIM
    lane256 = lax.broadcasted_iota(I32, (BLK, 2 * LANES), 1)
    head_masks = [(lane256 >= h * HEAD_DIM) & (lane256 < (h + 1) * HEAD_DIM) for h in range(Q_PER_KV)]
    key = lax.broadcasted_iota(I32, (2 * BLK, Q_PER_KV * BLK), 0)
    qry = lax.broadcasted_iota(I32, (2 * BLK, Q_PER_KV * BLK), 1) & (BLK - 1)
    in_window = (key > qry) & (key <= qry + BLK)
    first_ok = in_window & ((key >= BLK) | (j > 0))
    orow = lax.broadcasted_iota(I32, (2 * LANES, BLK), 0)
    out_masks = [(orow >= h * HEAD_DIM) & (orow < (h + 1) * HEAD_DIM) for h in range(Q_PER_KV)]

    trow = lax.broadcasted_iota(I32, (BLK, 2 * BLK), 0)
    tcol = lax.broadcasted_iota(I32, (BLK, 2 * BLK), 1) & (BLK - 1)
    causal2 = tcol <= trow
    wsp = [jnp.where(causal2, wsp_ref[p], 0.0).astype(BF16) for p in range(GMLP_WIDTH // LANES)]

    scale = F32(HEAD_DIM ** -0.5)
    k4_prev = [k4p_s[kvh] for kvh in range(N_KV_HEADS)]
    v4t_prev = [v4p_s[kvh] for kvh in range(N_KV_HEADS)]
    for i in range(tm // BLK):
        r0 = i * BLK
        kk = proj_s[r0:r0 + BLK, K_OFF:K_OFF + KV_WIDTH]
        vv = proj_s[r0:r0 + BLK, V_OFF:V_OFF + KV_WIDTH]
        kk_r = pltpu.roll(kk, HEAD_DIM, 1)
        vv_r = pltpu.roll(vv, HEAD_DIM, 1)
        o_parts = []
        for kvh in range(N_KV_HEADS):
            k2 = jnp.where(lo_half, kk, kk_r) if kvh == 0 else jnp.where(lo_half, kk_r, kk)
            v2 = jnp.where(lo_half, vv, vv_r) if kvh == 0 else jnp.where(lo_half, vv_r, vv)
            k4 = jnp.concatenate([k2, k2], axis=1).astype(BF16)
            v4t = jnp.transpose(jnp.concatenate([v2, v2], axis=1)).astype(BF16)
            kband = jnp.concatenate([k4_prev[kvh], k4], axis=0)
            vband_t = jnp.concatenate([v4t_prev[kvh], v4t], axis=1)
            k4_prev[kvh] = k4
            v4t_prev[kvh] = v4t
            q4 = proj_s[r0:r0 + BLK, Q_OFF + kvh * 256:Q_OFF + (kvh + 1) * 256] * scale
            qs = jnp.concatenate([jnp.where(m, q4, 0.0) for m in head_masks], axis=0).astype(BF16)
            st = lax.dot_general(kband, qs, (((1,), (1,)), ((), ())),
                                 preferred_element_type=F32)
            st = jnp.where(first_ok if i == 0 else in_window, st, NEG_INF)
            sink = jnp.concatenate(
                [jnp.full((1, BLK), sinks_ref[kvh * Q_PER_KV + h], F32) for h in range(Q_PER_KV)], axis=1)
            m = jnp.maximum(jnp.max(st, axis=0, keepdims=True), sink)
            p = jnp.exp(st - m)
            denom = jnp.sum(p, axis=0, keepdims=True) + jnp.exp(sink - m)
            probs_t = (p * (1.0 / denom)).astype(BF16)
            pvt = jnp.dot(vband_t, probs_t, preferred_element_type=F32)
            zt = jnp.where(out_masks[0], pvt[:, 0:BLK], 0.0)
            for h in range(1, Q_PER_KV):
                zt = zt + jnp.where(out_masks[h], pvt[:, h * BLK:(h + 1) * BLK], 0.0)
            o_parts.append(jnp.transpose(zt))
        a = jnp.concatenate(o_parts, axis=1)
        mixed_s[r0:r0 + BLK, 0:ATTN_WIDTH] = _rms(a, ag_ref[...]).astype(BF16)

        u = _gelu(proj_s[r0:r0 + BLK, GU_OFF:GU_OFF + GMLP_WIDTH])
        vg = _gelu(proj_s[r0:r0 + BLK, GV_OFF:GV_OFF + GMLP_WIDTH])
        mu = jnp.mean(vg, axis=-1, keepdims=True)
        var = jnp.mean(jnp.square(vg - mu), axis=-1, keepdims=True)
        vln = (vg - mu) * lax.rsqrt(var + LN_EPS) * lng_ref[...] + lnb_ref[...]
        s_parts = []
        for pidx in range(GMLP_WIDTH // LANES):
            vcol = vln[:, pidx * LANES:(pidx + 1) * LANES]
            rhs = jnp.concatenate([jnp.where(lo_half, vcol, 0.0), jnp.where(lo_half, 0.0, vcol)],
                                  axis=0).astype(BF16)
            s_parts.append(jnp.dot(wsp[pidx], rhs, preferred_element_type=F32))
        sg = jnp.concatenate(s_parts, axis=1) + bsp_ref[...]
        g = u * sg
        mixed_s[r0:r0 + BLK, ATTN_WIDTH:D_MODEL] = _rms(g, gg_ref[...]).astype(BF16)

    for kvh in range(N_KV_HEADS):
        k4p_s[kvh] = k4_prev[kvh]
        v4p_s[kvh] = v4t_prev[kvh]
    h_ref[...] = x_ref[...] + jnp.dot(mixed_s[...], wout_ref[...], preferred_element_type=F32)


def _mix(x2, sinks, g1, w_in, wsp, bsp, lng, lnb, ag, gg, w_out, batch, seq):
    t = x2.shape[0]
    tm = TM_MIX
    nj = seq // tm
    full = lambda shape: pl.BlockSpec(shape, lambda b, j: (0,) * len(shape))
    return pl.pallas_call(
        _mix_kernel,
        grid=(batch, nj),
        in_specs=[
            pl.BlockSpec(memory_space=pltpu.SMEM),
            pl.BlockSpec((tm, D_MODEL), lambda b, j: (b * nj + j, 0)),
            full((1, D_MODEL)),
            full((D_MODEL, IN_PROJ)),
            full((GMLP_WIDTH // LANES, BLK, 2 * BLK)),
            full((BLK, GMLP_WIDTH)),
            full((1, GMLP_WIDTH)),
            full((1, GMLP_WIDTH)),
            full((1, ATTN_WIDTH)),
            full((1, GMLP_WIDTH)),
            full((D_MODEL, D_MODEL)),
        ],
        out_specs=pl.BlockSpec((tm, D_MODEL), lambda b, j: (b * nj + j, 0)),
        out_shape=jax.ShapeDtypeStruct((t, D_MODEL), F32),
        scratch_shapes=[
            pltpu.VMEM((tm, IN_PROJ), F32),
            pltpu.VMEM((tm, D_MODEL), BF16),
            pltpu.VMEM((N_KV_HEADS, BLK, 2 * LANES), BF16),
            pltpu.VMEM((N_KV_HEADS, 2 * LANES, BLK), BF16),
        ],
        compiler_params=pltpu.CompilerParams(
            dimension_semantics=("arbitrary", "arbitrary"), vmem_limit_bytes=VMEM_LIMIT),
        name="mix",
    )(sinks, x2, g1, w_in, wsp, bsp, lng, lnb, ag, gg, w_out)


INFO_W1, INFO_W2, INFO_E1, INFO_E2, INFO_R1, INFO_R2 = range(6)


def _route_kernel(h_ref, g_ref, whi_ref, wlo_ref, b_ref, info_ref, wts_ref, cnt_ref, carry_s):
    i = pl.program_id(0)
    tb = h_ref.shape[0]

    @pl.when(i == 0)
    def _():
        carry_s[...] = jnp.zeros_like(carry_s)

    hn = _rms(h_ref[...], g_ref[...])
    hi = hn.astype(BF16)
    lo = (hn - hi.astype(F32)).astype(BF16)
    logits = (jnp.dot(hi, whi_ref[...], preferred_element_type=F32)
              + (jnp.dot(hi, wlo_ref[...], preferred_element_type=F32)
                 + jnp.dot(lo, whi_ref[...], preferred_element_type=F32))
              + b_ref[...])
    lt = jnp.transpose(logits)
    row8 = lax.broadcasted_iota(I32, (SUBLANES, tb), 0)
    ninf = F32(-jnp.inf)

    def top_row(vals):
        best = jnp.max(vals, axis=0, keepdims=True)
        return best, jnp.min(jnp.where(vals == best, row8, SUBLANES), axis=0, keepdims=True)

    gl = jnp.where(row8 < N_GROUPS, lt[GROUP_ROW0:GROUP_ROW0 + SUBLANES], ninf)
    gmax, gidx = top_row(gl)
    g_w = 1.0 / jnp.sum(jnp.exp(gl - gmax), axis=0, keepdims=True)

    el = lt[EXPERT_ROW0:EXPERT_ROW0 + E_PER_GROUP]
    for g in range(1, N_GROUPS):
        el = jnp.where(gidx == g, lt[EXPERT_ROW0 + g * E_PER_GROUP:EXPERT_ROW0 + (g + 1) * E_PER_GROUP], el)
    v1, i1 = top_row(el)
    v2, i2 = top_row(jnp.where(row8 == i1, ninf, el))
    t2 = jnp.exp(v2 - v1)
    inv = 1.0 / (1.0 + t2)
    w1 = inv * g_w
    w2 = (t2 * inv) * g_w
    e1 = gidx * E_PER_GROUP + i1
    e2 = gidx * E_PER_GROUP + i2

    rowe = lax.broadcasted_iota(I32, (N_EXPERTS, tb), 0)
    sel1 = rowe == e1
    sel2 = rowe == e2
    oh = jnp.where(sel1 | sel2, 1.0, 0.0)
    rr = lax.broadcasted_iota(I32, (tb, tb), 0)
    cc = lax.broadcasted_iota(I32, (tb, tb), 1)
    triu = jnp.where(rr <= cc, 1.0, 0.0).astype(BF16)
    incl = jnp.dot(oh.astype(BF16), triu, preferred_element_type=F32)
    carry = carry_s[...]
    before = incl - oh + carry[:, 0:1]
    rank1 = jnp.sum(jnp.where(sel1, before, 0.0), axis=0, keepdims=True)
    rank2 = jnp.sum(jnp.where(sel2, before, 0.0), axis=0, keepdims=True)
    carry = carry + incl[:, tb - 1:tb]
    carry_s[...] = carry
    cnt_ref[...] = carry

    rows = (w1, w2, e1.astype(F32), e2.astype(F32), rank1, rank2)
    info = jnp.zeros((SUBLANES, tb), F32)
    for k, val in enumerate(rows):
        info = jnp.where(row8 == k, val, info)
    info_ref[0] = info
    row128 = lax.broadcasted_iota(I32, (LANES, tb), 0)
    wts_ref[...] = jnp.transpose(jnp.where(row128 == INFO_W1, w1, jnp.where(row128 == INFO_W2, w2, 0.0)))


def _route(h, g, whi, wlo, bias):
    t = h.shape[0]
    tb = TB_ROUTE
    nsteps = t // tb
    full = lambda shape: pl.BlockSpec(shape, lambda i: (0,) * len(shape))
    return pl.pallas_call(
        _route_kernel,
        grid=(nsteps,),
        in_specs=[
            pl.BlockSpec((tb, D_MODEL), lambda i: (i, 0)),
            full((1, D_MODEL)),
            full((D_MODEL, LANES)),
            full((D_MODEL, LANES)),
            full((1, LANES)),
        ],
        out_specs=[
            pl.BlockSpec((1, SUBLANES, tb), lambda i: (i, 0, 0)),
            pl.BlockSpec((tb, LANES), lambda i: (i, 0)),
            full((N_EXPERTS, LANES)),
        ],
        out_shape=[
            jax.ShapeDtypeStruct((nsteps, SUBLANES, tb), F32),
            jax.ShapeDtypeStruct((t, LANES), F32),
            jax.ShapeDtypeStruct((N_EXPERTS, LANES), F32),
        ],
        scratch_shapes=[pltpu.VMEM((N_EXPERTS, LANES), F32)],
        compiler_params=pltpu.CompilerParams(
            dimension_semantics=("arbitrary",), vmem_limit_bytes=VMEM_LIMIT),
        name="route",
    )(h, g, whi, wlo, bias)


def _dispatch_kernel(cnt_ref, off_ref, nt_ref, h_ref, g_ref, info_ref, offcol_ref, xs_hbm, dst_ref,
                     rows_s, zeros_s, dstv_s, dsts_s, sems, fill_sem, *, n_tiles):
    i = pl.program_id(0)
    n_steps = pl.num_programs(0)
    tb = h_ref.shape[0]
    slot = i % 2

    def row_copy(buf, t, dst_row):
        return pltpu.make_async_copy(rows_s.at[buf, pl.ds(t * CHUNKS, CHUNKS), :], xs_hbm.at[dst_row],
                                     sems.at[buf])

    def drain_tile(buf):
        for _ in range(2 * tb):
            row_copy(buf, 0, 0).wait()

    @pl.when(i >= 2)
    def _():
        drain_tile(slot)

    _store_token_tiles(rows_s.at[slot], _rms(h_ref[...], g_ref[...]))

    info = info_ref[0]
    rowe = lax.broadcasted_iota(I32, (N_EXPERTS, tb), 0).astype(F32)
    off = offcol_ref[...][:, 0:1]
    dst_rows = []
    for ke, kr in ((INFO_E1, INFO_R1), (INFO_E2, INFO_R2)):
        base = jnp.sum(jnp.where(rowe == info[ke:ke + 1, :], off, 0.0), axis=0, keepdims=True)
        dst_rows.append(base + info[kr:kr + 1, :])
    row8 = lax.broadcasted_iota(I32, (SUBLANES, tb), 0)
    dst_t = jnp.where(row8 == 0, dst_rows[0], jnp.where(row8 == 1, dst_rows[1], 0.0)).astype(I32)
    dstv_s[...] = dst_t
    dst_ref[0] = dst_t
    pltpu.sync_copy(dstv_s, dsts_s)

    for t in range(tb):
        row_copy(slot, t, dsts_s[0, t]).start(priority=0)
        row_copy(slot, t, dsts_s[1, t]).start(priority=1)

    @pl.when(i == n_steps - 1)
    def _():
        zeros_s[...] = jnp.zeros_like(zeros_s)
        half = TE // 2

        def fill_copy(pos, size):
            return pltpu.make_async_copy(zeros_s.at[pl.ds(0, size)], xs_hbm.at[pl.ds(pos, size)], fill_sem)

        def for_each_gap(action):
            for e in range(N_EXPERTS):
                n = cnt_ref[e]
                gap = (-n) & (TE - 1)
                pos = off_ref[e] + n
                size = half
                while size >= 1:
                    @pl.when((gap & size) != 0)
                    def _(pos=pos, size=size):
                        action(fill_copy(pos, size))
                    pos = pos + (gap & size)
                    size //= 2

        def tail_issue(k, c):
            fill_copy(k * half, half).start()
            return c

        def tail_wait(k, c):
            fill_copy(0, half).wait()
            return c

        for_each_gap(lambda cp: cp.start())
        lax.fori_loop(2 * nt_ref[0], 2 * n_tiles, tail_issue, 0)
        for_each_gap(lambda cp: cp.wait())
        lax.fori_loop(2 * nt_ref[0], 2 * n_tiles, tail_wait, 0)
        drain_tile(slot)

        @pl.when(i >= 1)
        def _():
            drain_tile(1 - slot)


def _dispatch(counts, offsets, nt, h, g, info, off_col, n_tiles):
    t = h.shape[0]
    tb = TB_ROUTE
    nsteps = t // tb
    full = lambda shape: pl.BlockSpec(shape, lambda i, *_: (0,) * len(shape))
    return pl.pallas_call(
        functools.partial(_dispatch_kernel, n_tiles=n_tiles),
        grid_spec=pltpu.PrefetchScalarGridSpec(
            num_scalar_prefetch=3,
            grid=(nsteps,),
            in_specs=[
                pl.BlockSpec((tb, D_MODEL), lambda i, *_: (i, 0)),
                full((1, D_MODEL)),
                pl.BlockSpec((1, SUBLANES, tb), lambda i, *_: (i, 0, 0)),
                full((N_EXPERTS, LANES)),
            ],
            out_specs=[
                pl.BlockSpec(memory_space=pl.ANY),
                pl.BlockSpec((1, 8, tb), lambda i, *_: (i, 0, 0)),
            ],
            scratch_shapes=[
                pltpu.VMEM((2, tb * CHUNKS, LANES), F32),
                pltpu.VMEM((TE // 2, CHUNKS, LANES), F32),
                pltpu.VMEM((8, tb), I32),
                pltpu.SMEM((8, tb), I32),
                pltpu.SemaphoreType.DMA((2,)),
                pltpu.SemaphoreType.DMA,
            ],
        ),
        out_shape=[
            jax.ShapeDtypeStruct((n_tiles * TE, CHUNKS, LANES), F32),
            jax.ShapeDtypeStruct((nsteps, 8, tb), I32),
        ],
        compiler_params=pltpu.CompilerParams(
            dimension_semantics=("arbitrary",), vmem_limit_bytes=VMEM_LIMIT),
        name="dispatch",
    )(counts, offsets, nt, h, g, info, off_col)


def _expert_kernel(te_ref, nt_ref, x_ref, wg_ref, wu_ref, wd_ref, y_ref, wgu_s, wdb_s):
    i = pl.program_id(0)

    @pl.when(i >= nt_ref[0])
    def _():
        y_ref[...] = jnp.zeros_like(y_ref)

    @pl.when(i < nt_ref[0])
    def _():
        prev = te_ref[jnp.maximum(i - 1, 0)]

        @pl.when((i == 0) | (te_ref[i] != prev))
        def _():
            wgu_s[:, 0:D_EXPERT] = wg_ref[0].astype(BF16)
            wgu_s[:, D_EXPERT:2 * D_EXPERT] = wu_ref[0].astype(BF16)
            wdb_s[...] = wd_ref[0].astype(BF16)

        x = _load_token_tiles(x_ref, TE).astype(BF16)
        ab = jnp.dot(x, wgu_s[...], preferred_element_type=F32)
        a = ab[:, 0:D_EXPERT]
        b = ab[:, D_EXPERT:2 * D_EXPERT]
        hid = (a * jax.nn.sigmoid(a)) * b
        y = jnp.dot(hid.astype(BF16), wdb_s[...], preferred_element_type=F32)
        _store_token_tiles(y_ref, y)


def _experts(te, nt, xs, w_gate, w_up, w_down, n_tiles):
    tile = lambda i, te, nt: (i, 0)
    wblk = lambda i, te, nt: (te[i], 0, 0)
    return pl.pallas_call(
        _expert_kernel,
        grid_spec=pltpu.PrefetchScalarGridSpec(
            num_scalar_prefetch=2,
            grid=(n_tiles,),
            in_specs=[
                pl.BlockSpec((TE * CHUNKS, LANES), tile),
                pl.BlockSpec((1, D_MODEL, D_EXPERT), wblk),
                pl.BlockSpec((1, D_MODEL, D_EXPERT), wblk),
                pl.BlockSpec((1, D_EXPERT, D_MODEL), wblk),
            ],
            out_specs=pl.BlockSpec((TE * CHUNKS, LANES), tile),
            scratch_shapes=[
                pltpu.VMEM((D_MODEL, 2 * D_EXPERT), BF16),
                pltpu.VMEM((D_EXPERT, D_MODEL), BF16),
            ],
        ),
        out_shape=jax.ShapeDtypeStruct(xs.shape, F32),
        compiler_params=pltpu.CompilerParams(
            dimension_semantics=("arbitrary",), vmem_limit_bytes=VMEM_LIMIT),
        name="experts",
    )(te, nt, xs, w_gate, w_up, w_down)


def _combine_kernel(dst0_ref, dstn_ref, h_ref, wts_ref, g_ref, ys_hbm, out_ref, dsts_s, ybuf_s, sem):
    i = pl.program_id(0)
    n_steps = pl.num_programs(0)
    tb = h_ref.shape[0]
    slot = i % 2

    def row_copy(buf, k, t, src_row):
        return pltpu.make_async_copy(ys_hbm.at[src_row], ybuf_s.at[buf, k, pl.ds(t * CHUNKS, CHUNKS), :], sem)

    def issue_tile(buf):
        for t in range(tb):
            row_copy(buf, 0, t, dsts_s[0, t]).start(priority=0)
            row_copy(buf, 1, t, dsts_s[1, t]).start(priority=1)

    def drain_tile():
        for _ in range(2 * tb):
            row_copy(0, 0, 0, 0).wait()

    @pl.when(i == 0)
    def _():
        pltpu.sync_copy(dst0_ref.at[0], dsts_s)
        issue_tile(0)

    pltpu.sync_copy(dstn_ref.at[0], dsts_s)
    drain_tile()
    issue_tile(1 - slot)

    w = wts_ref[...]
    moe = (w[:, 0:1] * _load_token_tiles(ybuf_s.at[slot, 0], tb)
           + w[:, 1:2] * _load_token_tiles(ybuf_s.at[slot, 1], tb))
    out_ref[...] = _rms(h_ref[...] + moe, g_ref[...])

    @pl.when(i == n_steps - 1)
    def _():
        drain_tile()


def _combine(dst, h, wts, g, ys):
    t = h.shape[0]
    tb = TB_COMB
    n_steps = t // tb
    return pl.pallas_call(
        _combine_kernel,
        grid=(n_steps,),
        in_specs=[
            pl.BlockSpec((1, 8, tb), lambda i: (0, 0, 0)),
            pl.BlockSpec((1, 8, tb), lambda i: (jnp.minimum(i + 1, n_steps - 1), 0, 0)),
            pl.BlockSpec((tb, D_MODEL), lambda i: (i, 0)),
            pl.BlockSpec((tb, LANES), lambda i: (i, 0)),
            pl.BlockSpec((1, D_MODEL), lambda i: (0, 0)),
            pl.BlockSpec(memory_space=pl.ANY),
        ],
        out_specs=pl.BlockSpec((tb, D_MODEL), lambda i: (i, 0)),
        out_shape=jax.ShapeDtypeStruct((t, D_MODEL), F32),
        scratch_shapes=[
            pltpu.SMEM((8, tb), I32),
            pltpu.VMEM((2, 2, tb * CHUNKS, LANES), F32),
            pltpu.SemaphoreType.DMA,
        ],
        compiler_params=pltpu.CompilerParams(
            dimension_semantics=("arbitrary",), vmem_limit_bytes=VMEM_LIMIT),
        name="combine",
    )(dst, dst, h, wts, g, ys)


def _tile_tables(counts, n_tiles):
    per = (counts + (TE - 1)) // TE
    cum = jnp.cumsum(per)
    total = cum[-1]
    offsets = ((cum - per) * TE).astype(I32)
    idx = jnp.minimum(jnp.arange(n_tiles, dtype=I32), total - 1)
    te = jnp.minimum(jnp.sum((idx[:, None] >= cum[None, :]).astype(I32), axis=1), N_EXPERTS - 1)
    return offsets, te, total.reshape(1).astype(I32)


def kernel(x, mix_norm_g, w_in, attn_sinks, w_spatial, b_spatial, gmlp_ln_g, gmlp_ln_b, attn_out_g, gmlp_out_g, w_out, ffn_norm_g, w_group_router, b_group_router, w_expert_router, b_expert_router, w_gate, w_up, w_down, final_norm_g):
    batch, seq, _ = x.shape
    t = batch * seq
    depth = mix_norm_g.shape[0]
    assert depth == 1, "the final norm is fused into the last layer's combine kernel"
    n_tiles = (2 * t) // TE + N_EXPERTS
    h = x.reshape(t, D_MODEL)
    for l in range(depth):
        n_pairs = GMLP_WIDTH // LANES
        wsp = w_spatial[l].reshape(n_pairs, 2, BLK, BLK).transpose(0, 2, 1, 3).reshape(n_pairs, BLK, 2 * BLK)
        bsp = jnp.repeat(b_spatial[l].T, HEAD_DIM, axis=1)
        h = _mix(h, attn_sinks[l], mix_norm_g[l][None], w_in[l].astype(BF16), wsp, bsp,
                 gmlp_ln_g[l][None], gmlp_ln_b[l][None], attn_out_g[l][None], gmlp_out_g[l][None],
                 w_out[l].astype(BF16), batch, seq)

        w_r = jnp.concatenate(
            [w_group_router[l], jnp.zeros((D_MODEL, EXPERT_ROW0 - N_GROUPS), F32),
             w_expert_router[l].transpose(1, 0, 2).reshape(D_MODEL, N_EXPERTS)], axis=1)
        w_r = jnp.pad(w_r, ((0, 0), (0, LANES - w_r.shape[1])))
        b_r = jnp.concatenate([b_group_router[l], jnp.zeros((EXPERT_ROW0 - N_GROUPS,), F32),
                               b_expert_router[l].reshape(-1)])
        b_r = jnp.pad(b_r, (0, LANES - b_r.shape[0]))[None]
        w_r_top = lax.bitcast_convert_type(
            lax.bitcast_convert_type(w_r, U32) & U32(0xFFFF0000), F32)
        w_r_hi = w_r_top.astype(BF16)
        w_r_lo = (w_r - w_r_top).astype(BF16)
        info, wts, cnt = _route(h, ffn_norm_g[l][None], w_r_hi, w_r_lo, b_r)
        counts = cnt[:, 0].astype(I32)
        offsets, te, nt = _tile_tables(counts, n_tiles)
        off_col = jnp.broadcast_to(offsets.astype(F32)[:, None], (N_EXPERTS, LANES))
        xs, dst = _dispatch(counts, offsets, nt, h, ffn_norm_g[l][None], info, off_col, n_tiles)
        ys = _experts(te, nt, xs.reshape(-1, LANES), w_gate[l], w_up[l], w_down[l], n_tiles)
        h = _combine(dst, h, wts, final_norm_g[None], ys.reshape(xs.shape))
    return h.reshape(batch, seq, D_MODEL)
```

```python
import functools

import jax
import jax.numpy as jnp
from jax import lax
from jax.experimental import pallas as pl
from jax.experimental.pallas import tpu as pltpu

F32 = jnp.float32
BF16 = jnp.bfloat16
I32 = jnp.int32
U32 = jnp.uint32

D_MODEL = 1024
HEAD_DIM = 64
ATTN_WIDTH = 512
N_Q_HEADS = 8
N_KV_HEADS = 2
Q_PER_KV = 4
KV_WIDTH = 128
BLK = 128
GMLP_WIDTH = 512
IN_PROJ = 1792
N_GROUPS = 4
E_PER_GROUP = 8
N_EXPERTS = 32
D_EXPERT = 256
RMS_EPS = 1e-6
LN_EPS = 1e-5
NEG_INF = -1e30

LANES = 128
SUBLANES = 8
CHUNKS = D_MODEL // LANES
TM_MIX = 512
TB_ROUTE = 512
TE = 512
TB_COMB = 512
COMB_ROWS = 32
DISP_ROWS = 32
VMEM_LIMIT = 56 * 1024 * 1024

Q_OFF, K_OFF, V_OFF = 0, 512, 640
GU_OFF, GV_OFF = 768, 1280
GROUP_ROW0, EXPERT_ROW0 = 0, 8


def _rms(x, g):
    return (x * lax.rsqrt(jnp.mean(x * x, axis=-1, keepdims=True) + RMS_EPS)) * g


def _gelu(x):
    return 0.5 * x * (1.0 + lax.erf(x * F32(0.7071067811865476)))


def _store_token_tiles(ref2d, val):
    n = val.shape[0]
    for i in range(n // SUBLANES):
        for c in range(CHUNKS):
            ref2d[pl.ds(i * SUBLANES * CHUNKS + c, SUBLANES, stride=CHUNKS), :] = (
                val[i * SUBLANES:(i + 1) * SUBLANES, c * LANES:(c + 1) * LANES])


def _load_token_tiles(ref2d, n):
    cols = []
    for c in range(CHUNKS):
        cols.append(jnp.concatenate(
            [ref2d[pl.ds(i * SUBLANES * CHUNKS + c, SUBLANES, stride=CHUNKS), :] for i in range(n // SUBLANES)],
            axis=0))
    return jnp.concatenate(cols, axis=1)


def _mix_kernel(sinks_ref, x_ref, g1_ref, win_ref, wsp_ref, bsp_ref, lng_ref, lnb_ref,
                ag_ref, gg_ref, wout_ref, h_ref, proj_s, mixed_s, k4p_s, v4p_s):
    j = pl.program_id(1)
    tm = x_ref.shape[0]

    @pl.when(j == 0)
    def _():
        k4p_s[...] = jnp.zeros_like(k4p_s)
        v4p_s[...] = jnp.zeros_like(v4p_s)

    xn = _rms(x_ref[...], g1_ref[...])
    proj_s[...] = jnp.dot(xn.astype(BF16), win_ref[...], preferred_element_type=F32)

    lane128 = lax.broadcasted_iota(I32, (BLK, LANES), 1)
    lo_half = lane128 < HEAD_DIM
    lane256 = lax.broadcasted_iota(I32, (BLK, 2 * LANES), 1)
    head_masks = [(lane256 >= h * HEAD_DIM) & (lane256 < (h + 1) * HEAD_DIM) for h in range(Q_PER_KV)]
    key = lax.broadcasted_iota(I32, (2 * BLK, Q_PER_KV * BLK), 0)
    qry = lax.broadcasted_iota(I32, (2 * BLK, Q_PER_KV * BLK), 1) & (BLK - 1)
    in_window = (key > qry) & (key <= qry + BLK)
    first_ok = in_window & ((key >= BLK) | (j > 0))

    trow = lax.broadcasted_iota(I32, (BLK, 2 * BLK), 0)
    tcol = lax.broadcasted_iota(I32, (BLK, 2 * BLK), 1) & (BLK - 1)
    causal2 = tcol <= trow
    wsp = [jnp.where(causal2, wsp_ref[p], 0.0).astype(BF16) for p in range(GMLP_WIDTH // LANES)]

    scale = F32(HEAD_DIM ** -0.5)
    k4_prev = [k4p_s[kvh] for kvh in range(N_KV_HEADS)]
    v4t_prev = [v4p_s[kvh] for kvh in range(N_KV_HEADS)]
    for i in range(tm // BLK):
        r0 = i * BLK
        kk = proj_s[r0:r0 + BLK, K_OFF:K_OFF + KV_WIDTH]
        vv = proj_s[r0:r0 + BLK, V_OFF:V_OFF + KV_WIDTH]
        kk_r = pltpu.roll(kk, HEAD_DIM, 1)
        vv_r = pltpu.roll(vv, HEAD_DIM, 1)
        o_parts = []
        for kvh in range(N_KV_HEADS):
            k2 = jnp.where(lo_half, kk, kk_r) if kvh == 0 else jnp.where(lo_half, kk_r, kk)
            v2 = jnp.where(lo_half, vv, vv_r) if kvh == 0 else jnp.where(lo_half, vv_r, vv)
            k4 = jnp.concatenate([k2, k2], axis=1).astype(BF16)
            v4t = jnp.transpose(jnp.concatenate([v2, v2], axis=1)).astype(BF16)
            kband = jnp.concatenate([k4_prev[kvh], k4], axis=0)
            vband_t = jnp.concatenate([v4t_prev[kvh], v4t], axis=1)
            k4_prev[kvh] = k4
            v4t_prev[kvh] = v4t
            q4 = proj_s[r0:r0 + BLK, Q_OFF + kvh * 256:Q_OFF + (kvh + 1) * 256] * scale
            qs = jnp.concatenate([jnp.where(m, q4, 0.0) for m in head_masks], axis=0).astype(BF16)
            st = lax.dot_general(kband, qs, (((1,), (1,)), ((), ())),
                                 preferred_element_type=F32)
            st = jnp.where(first_ok if i == 0 else in_window, st, NEG_INF)
            sink = jnp.concatenate(
                [jnp.full((1, BLK), sinks_ref[kvh * Q_PER_KV + h], F32) for h in range(Q_PER_KV)], axis=1)
            m = jnp.maximum(jnp.max(st, axis=0, keepdims=True), sink)
            p = jnp.exp(st - m)
            denom = jnp.sum(p, axis=0, keepdims=True) + jnp.exp(sink - m)
            probs_t = (p * (1.0 / denom)).astype(BF16)
            pvt = jnp.dot(vband_t, probs_t, preferred_element_type=F32)
            zt = jnp.concatenate([pvt[h * HEAD_DIM:(h + 1) * HEAD_DIM, h * BLK:(h + 1) * BLK]
                                  for h in range(Q_PER_KV)], axis=0)
            o_parts.append(jnp.transpose(zt))
        a = jnp.concatenate(o_parts, axis=1)
        mixed_s[r0:r0 + BLK, 0:ATTN_WIDTH] = _rms(a, ag_ref[...]).astype(BF16)

        u = _gelu(proj_s[r0:r0 + BLK, GU_OFF:GU_OFF + GMLP_WIDTH])
        vg = _gelu(proj_s[r0:r0 + BLK, GV_OFF:GV_OFF + GMLP_WIDTH])
        mu = jnp.mean(vg, axis=-1, keepdims=True)
        var = jnp.mean(jnp.square(vg - mu), axis=-1, keepdims=True)
        vln = (vg - mu) * lax.rsqrt(var + LN_EPS) * lng_ref[...] + lnb_ref[...]
        s_parts = []
        for pidx in range(GMLP_WIDTH // LANES):
            vcol = vln[:, pidx * LANES:(pidx + 1) * LANES]
            rhs = jnp.concatenate([jnp.where(lo_half, vcol, 0.0), jnp.where(lo_half, 0.0, vcol)],
                                  axis=0).astype(BF16)
            s_parts.append(jnp.dot(wsp[pidx], rhs, preferred_element_type=F32))
        sg = jnp.concatenate(s_parts, axis=1) + bsp_ref[...]
        g = u * sg
        mixed_s[r0:r0 + BLK, ATTN_WIDTH:D_MODEL] = _rms(g, gg_ref[...]).astype(BF16)

    for kvh in range(N_KV_HEADS):
        k4p_s[kvh] = k4_prev[kvh]
        v4p_s[kvh] = v4t_prev[kvh]
    h_ref[...] = x_ref[...] + jnp.dot(mixed_s[...], wout_ref[...], preferred_element_type=F32)


def _mix(x2, sinks, g1, w_in, wsp, bsp, lng, lnb, ag, gg, w_out, batch, seq):
    t = x2.shape[0]
    tm = TM_MIX
    nj = seq // tm
    full = lambda shape: pl.BlockSpec(shape, lambda b, j: (0,) * len(shape))
    return pl.pallas_call(
        _mix_kernel,
        grid=(batch, nj),
        in_specs=[
            pl.BlockSpec(memory_space=pltpu.SMEM),
            pl.BlockSpec((tm, D_MODEL), lambda b, j: (b * nj + j, 0)),
            full((1, D_MODEL)),
            full((D_MODEL, IN_PROJ)),
            full((GMLP_WIDTH // LANES, BLK, 2 * BLK)),
            full((BLK, GMLP_WIDTH)),
            full((1, GMLP_WIDTH)),
            full((1, GMLP_WIDTH)),
            full((1, ATTN_WIDTH)),
            full((1, GMLP_WIDTH)),
            full((D_MODEL, D_MODEL)),
        ],
        out_specs=pl.BlockSpec((tm, D_MODEL), lambda b, j: (b * nj + j, 0)),
        out_shape=jax.ShapeDtypeStruct((t, D_MODEL), F32),
        scratch_shapes=[
            pltpu.VMEM((tm, IN_PROJ), F32),
            pltpu.VMEM((tm, D_MODEL), BF16),
            pltpu.VMEM((N_KV_HEADS, BLK, 2 * LANES), BF16),
            pltpu.VMEM((N_KV_HEADS, 2 * LANES, BLK), BF16),
        ],
        compiler_params=pltpu.CompilerParams(
            dimension_semantics=("arbitrary", "arbitrary"), vmem_limit_bytes=VMEM_LIMIT),
        name="mix",
    )(sinks, x2, g1, w_in, wsp, bsp, lng, lnb, ag, gg, w_out)


INFO_W1, INFO_W2, INFO_E1, INFO_E2, INFO_R1, INFO_R2 = range(6)


def _route_kernel(h_ref, g_ref, whi_ref, wlo_ref, b_ref, info_ref, wts_ref, cnt_ref, carry_s):
    i = pl.program_id(0)
    tb = h_ref.shape[0]

    @pl.when(i == 0)
    def _():
        carry_s[...] = jnp.zeros_like(carry_s)

    hn = _rms(h_ref[...], g_ref[...])
    hi = hn.astype(BF16)
    lo = (hn - hi.astype(F32)).astype(BF16)
    logits = (jnp.dot(hi, whi_ref[...], preferred_element_type=F32)
              + (jnp.dot(hi, wlo_ref[...], preferred_element_type=F32)
                 + jnp.dot(lo, whi_ref[...], preferred_element_type=F32))
              + b_ref[...])
    lt = jnp.transpose(logits)
    row8 = lax.broadcasted_iota(I32, (SUBLANES, tb), 0)
    ninf = F32(-jnp.inf)

    def top_row(vals):
        best = jnp.max(vals, axis=0, keepdims=True)
        return best, jnp.min(jnp.where(vals == best, row8, SUBLANES), axis=0, keepdims=True)

    gl = jnp.where(row8 < N_GROUPS, lt[GROUP_ROW0:GROUP_ROW0 + SUBLANES], ninf)
    gmax, gidx = top_row(gl)
    g_w = 1.0 / jnp.sum(jnp.exp(gl - gmax), axis=0, keepdims=True)

    el = lt[EXPERT_ROW0:EXPERT_ROW0 + E_PER_GROUP]
    for g in range(1, N_GROUPS):
        el = jnp.where(gidx == g, lt[EXPERT_ROW0 + g * E_PER_GROUP:EXPERT_ROW0 + (g + 1) * E_PER_GROUP], el)
    v1, i1 = top_row(el)
    v2, i2 = top_row(jnp.where(row8 == i1, ninf, el))
    t2 = jnp.exp(v2 - v1)
    inv = 1.0 / (1.0 + t2)
    w1 = inv * g_w
    w2 = (t2 * inv) * g_w
    e1 = gidx * E_PER_GROUP + i1
    e2 = gidx * E_PER_GROUP + i2

    rowe = lax.broadcasted_iota(I32, (N_EXPERTS, tb), 0)
    sel1 = rowe == e1
    sel2 = rowe == e2
    oh = jnp.where(sel1 | sel2, 1.0, 0.0)
    rr = lax.broadcasted_iota(I32, (tb, tb), 0)
    cc = lax.broadcasted_iota(I32, (tb, tb), 1)
    triu = jnp.where(rr <= cc, 1.0, 0.0).astype(BF16)
    incl = jnp.dot(oh.astype(BF16), triu, preferred_element_type=F32)
    carry = carry_s[...]
    before = incl - oh + carry[:, 0:1]
    rank1 = jnp.sum(jnp.where(sel1, before, 0.0), axis=0, keepdims=True)
    rank2 = jnp.sum(jnp.where(sel2, before, 0.0), axis=0, keepdims=True)
    carry = carry + incl[:, tb - 1:tb]
    carry_s[...] = carry
    cnt_ref[...] = carry

    rows = (w1, w2, e1.astype(F32), e2.astype(F32), rank1, rank2)
    info = jnp.zeros((SUBLANES, tb), F32)
    for k, val in enumerate(rows):
        info = jnp.where(row8 == k, val, info)
    info_ref[0] = info
    row128 = lax.broadcasted_iota(I32, (LANES, tb), 0)
    wts_ref[...] = jnp.transpose(jnp.where(row128 == INFO_W1, w1, jnp.where(row128 == INFO_W2, w2, 0.0)))


def _route(h, g, whi, wlo, bias):
    t = h.shape[0]
    tb = TB_ROUTE
    nsteps = t // tb
    full = lambda shape: pl.BlockSpec(shape, lambda i: (0,) * len(shape))
    return pl.pallas_call(
        _route_kernel,
        grid=(nsteps,),
        in_specs=[
            pl.BlockSpec((tb, D_MODEL), lambda i: (i, 0)),
            full((1, D_MODEL)),
            full((D_MODEL, LANES)),
            full((D_MODEL, LANES)),
            full((1, LANES)),
        ],
        out_specs=[
            pl.BlockSpec((1, SUBLANES, tb), lambda i: (i, 0, 0)),
            pl.BlockSpec((tb, LANES), lambda i: (i, 0)),
            full((N_EXPERTS, LANES)),
        ],
        out_shape=[
            jax.ShapeDtypeStruct((nsteps, SUBLANES, tb), F32),
            jax.ShapeDtypeStruct((t, LANES), F32),
            jax.ShapeDtypeStruct((N_EXPERTS, LANES), F32),
        ],
        scratch_shapes=[pltpu.VMEM((N_EXPERTS, LANES), F32)],
        compiler_params=pltpu.CompilerParams(
            dimension_semantics=("arbitrary",), vmem_limit_bytes=VMEM_LIMIT),
        name="route",
    )(h, g, whi, wlo, bias)


def _dispatch_kernel(cnt_ref, off_ref, nt_ref, h_ref, g_ref, info_ref, offcol_ref, xs_hbm, dst_ref,
                     rows_s, zeros_s, dstv_s, dsts_s, sems, fill_sem, *, n_tiles):
    i = pl.program_id(0)
    n_steps = pl.num_programs(0)
    tb = h_ref.shape[0]
    slot = i % 2

    def row_copy(buf, t, dst_row):
        return pltpu.make_async_copy(rows_s.at[buf, pl.ds(t * CHUNKS, CHUNKS), :], xs_hbm.at[dst_row],
                                     sems.at[buf])

    def drain_tile(buf):
        for _ in range(2 * tb):
            row_copy(buf, 0, 0).wait()

    @pl.when(i >= 2)
    def _():
        drain_tile(slot)

    info = info_ref[0]
    rowe = lax.broadcasted_iota(I32, (N_EXPERTS, tb), 0).astype(F32)
    off = offcol_ref[...][:, 0:1]
    dst_rows = []
    for ke, kr in ((INFO_E1, INFO_R1), (INFO_E2, INFO_R2)):
        base = jnp.sum(jnp.where(rowe == info[ke:ke + 1, :], off, 0.0), axis=0, keepdims=True)
        dst_rows.append(base + info[kr:kr + 1, :])
    row8 = lax.broadcasted_iota(I32, (SUBLANES, tb), 0)
    dst_t = jnp.where(row8 == 0, dst_rows[0], jnp.where(row8 == 1, dst_rows[1], 0.0)).astype(I32)
    dstv_s[...] = dst_t
    dst_ref[0] = dst_t
    pltpu.sync_copy(dstv_s, dsts_s)

    for r0 in range(0, tb, DISP_ROWS):
        _store_token_tiles(rows_s.at[slot, pl.ds(r0 * CHUNKS, DISP_ROWS * CHUNKS), :],
                           _rms(h_ref[r0:r0 + DISP_ROWS, :], g_ref[...]))
        for t in range(r0, r0 + DISP_ROWS):
            row_copy(slot, t, dsts_s[0, t]).start(priority=0)
            row_copy(slot, t, dsts_s[1, t]).start(priority=1)

    @pl.when(i == n_steps - 1)
    def _():
        zeros_s[...] = jnp.zeros_like(zeros_s)
        half = TE // 2

        def fill_copy(pos, size):
            return pltpu.make_async_copy(zeros_s.at[pl.ds(0, size)], xs_hbm.at[pl.ds(pos, size)], fill_sem)

        def for_each_gap(action):
            for e in range(N_EXPERTS):
                n = cnt_ref[e]
                gap = (-n) & (TE - 1)
                pos = off_ref[e] + n
                size = half
                while size >= 1:
                    @pl.when((gap & size) != 0)
                    def _(pos=pos, size=size):
                        action(fill_copy(pos, size))
                    pos = pos + (gap & size)
                    size //= 2

        def tail_issue(k, c):
            fill_copy(k * half, half).start()
            return c

        def tail_wait(k, c):
            fill_copy(0, half).wait()
            return c

        for_each_gap(lambda cp: cp.start())
        lax.fori_loop(2 * nt_ref[0], 2 * n_tiles, tail_issue, 0)
        for_each_gap(lambda cp: cp.wait())
        lax.fori_loop(2 * nt_ref[0], 2 * n_tiles, tail_wait, 0)
        drain_tile(slot)

        @pl.when(i >= 1)
        def _():
            drain_tile(1 - slot)


def _dispatch(counts, offsets, nt, h, g, info, off_col, n_tiles):
    t = h.shape[0]
    tb = TB_ROUTE
    nsteps = t // tb
    full = lambda shape: pl.BlockSpec(shape, lambda i, *_: (0,) * len(shape))
    return pl.pallas_call(
        functools.partial(_dispatch_kernel, n_tiles=n_tiles),
        grid_spec=pltpu.PrefetchScalarGridSpec(
            num_scalar_prefetch=3,
            grid=(nsteps,),
            in_specs=[
                pl.BlockSpec((tb, D_MODEL), lambda i, *_: (i, 0)),
                full((1, D_MODEL)),
                pl.BlockSpec((1, SUBLANES, tb), lambda i, *_: (i, 0, 0)),
                full((N_EXPERTS, LANES)),
            ],
            out_specs=[
                pl.BlockSpec(memory_space=pl.ANY),
                pl.BlockSpec((1, 8, tb), lambda i, *_: (i, 0, 0)),
            ],
            scratch_shapes=[
                pltpu.VMEM((2, tb * CHUNKS, LANES), F32),
                pltpu.VMEM((TE // 2, CHUNKS, LANES), F32),
                pltpu.VMEM((8, tb), I32),
                pltpu.SMEM((8, tb), I32),
                pltpu.SemaphoreType.DMA((2,)),
                pltpu.SemaphoreType.DMA,
            ],
        ),
        out_shape=[
            jax.ShapeDtypeStruct((n_tiles * TE, CHUNKS, LANES), F32),
            jax.ShapeDtypeStruct((nsteps, 8, tb), I32),
        ],
        compiler_params=pltpu.CompilerParams(
            dimension_semantics=("arbitrary",), vmem_limit_bytes=VMEM_LIMIT),
        name="dispatch",
    )(counts, offsets, nt, h, g, info, off_col)


def _expert_kernel(te_ref, nt_ref, x_ref, wg_ref, wu_ref, wd_ref, y_ref, wgu_s, wdb_s):
    i = pl.program_id(0)

    @pl.when(i >= nt_ref[0])
    def _():
        y_ref[...] = jnp.zeros_like(y_ref)

    @pl.when(i < nt_ref[0])
    def _():
        prev = te_ref[jnp.maximum(i - 1, 0)]

        @pl.when((i == 0) | (te_ref[i] != prev))
        def _():
            wgu_s[:, 0:D_EXPERT] = wg_ref[0].astype(BF16)
            wgu_s[:, D_EXPERT:2 * D_EXPERT] = wu_ref[0].astype(BF16)
            wdb_s[...] = wd_ref[0].astype(BF16)

        x = _load_token_tiles(x_ref, TE).astype(BF16)
        ab = jnp.dot(x, wgu_s[...], preferred_element_type=F32)
        a = ab[:, 0:D_EXPERT]
        b = ab[:, D_EXPERT:2 * D_EXPERT]
        hid = (a * jax.nn.sigmoid(a)) * b
        y = jnp.dot(hid.astype(BF16), wdb_s[...], preferred_element_type=F32)
        _store_token_tiles(y_ref, y)


def _experts(te, nt, xs, w_gate, w_up, w_down, n_tiles):
    tile = lambda i, te, nt: (i, 0)
    busy_tile = lambda i, te, nt: (jnp.minimum(i, nt[0] - 1), 0)
    wblk = lambda i, te, nt: (te[i], 0, 0)
    return pl.pallas_call(
        _expert_kernel,
        grid_spec=pltpu.PrefetchScalarGridSpec(
            num_scalar_prefetch=2,
            grid=(n_tiles,),
            in_specs=[
                pl.BlockSpec((TE * CHUNKS, LANES), busy_tile),
                pl.BlockSpec((1, D_MODEL, D_EXPERT), wblk),
                pl.BlockSpec((1, D_MODEL, D_EXPERT), wblk),
                pl.BlockSpec((1, D_EXPERT, D_MODEL), wblk),
            ],
            out_specs=pl.BlockSpec((TE * CHUNKS, LANES), tile),
            scratch_shapes=[
                pltpu.VMEM((D_MODEL, 2 * D_EXPERT), BF16),
                pltpu.VMEM((D_EXPERT, D_MODEL), BF16),
            ],
        ),
        out_shape=jax.ShapeDtypeStruct(xs.shape, F32),
        compiler_params=pltpu.CompilerParams(
            dimension_semantics=("arbitrary",), vmem_limit_bytes=VMEM_LIMIT),
        name="experts",
    )(te, nt, xs, w_gate, w_up, w_down)


def _combine_kernel(dst0_ref, dstn_ref, h_ref, wts_ref, g_ref, ys_hbm, out_ref, dsts_s, ybuf_s, sem):
    i = pl.program_id(0)
    n_steps = pl.num_programs(0)
    tb = h_ref.shape[0]
    slot = i % 2

    def row_copy(buf, k, t, src_row):
        return pltpu.make_async_copy(ys_hbm.at[src_row], ybuf_s.at[buf, k, pl.ds(t * CHUNKS, CHUNKS), :], sem)

    def issue_tile(buf):
        for t in range(tb):
            row_copy(buf, 0, t, dsts_s[0, t]).start(priority=0)
            row_copy(buf, 1, t, dsts_s[1, t]).start(priority=1)

    def drain_tile():
        for _ in range(2 * tb):
            row_copy(0, 0, 0, 0).wait()

    @pl.when(i == 0)
    def _():
        pltpu.sync_copy(dst0_ref.at[0], dsts_s)
        issue_tile(0)

    pltpu.sync_copy(dstn_ref.at[0], dsts_s)
    drain_tile()

    for r0 in range(0, tb, COMB_ROWS):
        rows = slice(r0, r0 + COMB_ROWS)
        part = [_load_token_tiles(ybuf_s.at[slot, k, pl.ds(r0 * CHUNKS, COMB_ROWS * CHUNKS), :], COMB_ROWS)
                for k in range(2)]
        w = wts_ref[rows, :]
        moe = w[:, INFO_W1:INFO_W1 + 1] * part[0] + w[:, INFO_W2:INFO_W2 + 1] * part[1]
        out_ref[rows, :] = _rms(h_ref[rows, :] + moe, g_ref[...])
        for t in range(r0, r0 + COMB_ROWS):
            row_copy(1 - slot, 0, t, dsts_s[0, t]).start(priority=0)
            row_copy(1 - slot, 1, t, dsts_s[1, t]).start(priority=1)

    @pl.when(i == n_steps - 1)
    def _():
        drain_tile()


def _combine(dst, h, wts, g, ys):
    t = h.shape[0]
    tb = TB_COMB
    n_steps = t // tb
    return pl.pallas_call(
        _combine_kernel,
        grid=(n_steps,),
        in_specs=[
            pl.BlockSpec((1, 8, tb), lambda i: (0, 0, 0)),
            pl.BlockSpec((1, 8, tb), lambda i: (jnp.minimum(i + 1, n_steps - 1), 0, 0)),
            pl.BlockSpec((tb, D_MODEL), lambda i: (i, 0)),
            pl.BlockSpec((tb, LANES), lambda i: (i, 0)),
            pl.BlockSpec((1, D_MODEL), lambda i: (0, 0)),
            pl.BlockSpec(memory_space=pl.ANY),
        ],
        out_specs=pl.BlockSpec((tb, D_MODEL), lambda i: (i, 0)),
        out_shape=jax.ShapeDtypeStruct((t, D_MODEL), F32),
        scratch_shapes=[
            pltpu.SMEM((8, tb), I32),
            pltpu.VMEM((2, 2, tb * CHUNKS, LANES), F32),
            pltpu.SemaphoreType.DMA,
        ],
        compiler_params=pltpu.CompilerParams(
            dimension_semantics=("arbitrary",), vmem_limit_bytes=VMEM_LIMIT),
        name="combine",
    )(dst, dst, h, wts, g, ys)


def _tile_tables(counts, n_tiles):
    per = (counts + (TE - 1)) // TE
    cum = jnp.cumsum(per)
    total = cum[-1]
    offsets = ((cum - per) * TE).astype(I32)
    idx = jnp.minimum(jnp.arange(n_tiles, dtype=I32), total - 1)
    te = jnp.minimum(jnp.sum((idx[:, None] >= cum[None, :]).astype(I32), axis=1), N_EXPERTS - 1)
    return offsets, te, total.reshape(1).astype(I32)


def kernel(x, mix_norm_g, w_in, attn_sinks, w_spatial, b_spatial, gmlp_ln_g, gmlp_ln_b, attn_out_g, gmlp_out_g, w_out, ffn_norm_g, w_group_router, b_group_router, w_expert_router, b_expert_router, w_gate, w_up, w_down, final_norm_g):
    batch, seq, _ = x.shape
    t = batch * seq
    depth = mix_norm_g.shape[0]
    assert depth == 1, "the final norm is fused into the last layer's combine kernel"
    n_tiles = (2 * t) // TE + N_EXPERTS
    h = x.reshape(t, D_MODEL)
    for l in range(depth):
        n_pairs = GMLP_WIDTH // LANES
        wsp = w_spatial[l].reshape(n_pairs, 2, BLK, BLK).transpose(0, 2, 1, 3).reshape(n_pairs, BLK, 2 * BLK)
        bsp = jnp.repeat(b_spatial[l].T, HEAD_DIM, axis=1)
        h = _mix(h, attn_sinks[l], mix_norm_g[l][None], w_in[l].astype(BF16), wsp, bsp,
                 gmlp_ln_g[l][None], gmlp_ln_b[l][None], attn_out_g[l][None], gmlp_out_g[l][None],
                 w_out[l].astype(BF16), batch, seq)

        w_r = jnp.concatenate(
            [w_group_router[l], jnp.zeros((D_MODEL, EXPERT_ROW0 - N_GROUPS), F32),
             w_expert_router[l].transpose(1, 0, 2).reshape(D_MODEL, N_EXPERTS)], axis=1)
        w_r = jnp.pad(w_r, ((0, 0), (0, LANES - w_r.shape[1])))
        b_r = jnp.concatenate([b_group_router[l], jnp.zeros((EXPERT_ROW0 - N_GROUPS,), F32),
                               b_expert_router[l].reshape(-1)])
        b_r = jnp.pad(b_r, (0, LANES - b_r.shape[0]))[None]
        w_r_top = lax.bitcast_convert_type(
            lax.bitcast_convert_type(w_r, U32) & U32(0xFFFF0000), F32)
        w_r_hi = w_r_top.astype(BF16)
        w_r_lo = (w_r - w_r_top).astype(BF16)
        info, wts, cnt = _route(h, ffn_norm_g[l][None], w_r_hi, w_r_lo, b_r)
        counts = cnt[:, 0].astype(I32)
        offsets, te, nt = _tile_tables(counts, n_tiles)
        off_col = jnp.broadcast_to(offsets.astype(F32)[:, None], (N_EXPERTS, LANES))
        xs, dst = _dispatch(counts, offsets, nt, h, ffn_norm_g[l][None], info, off_col, n_tiles)
        ys = _experts(te, nt, xs.reshape(-1, LANES), w_gate[l], w_up[l], w_down[l], n_tiles)
        h = _combine(dst, h, wts, final_norm_g[None], ys.reshape(xs.shape))
    return h.reshape(batch, seq, D_MODEL)
```

```python
import functools

import jax
import jax.numpy as jnp
from jax import lax
from jax.experimental import pallas as pl
from jax.experimental.pallas import tpu as pltpu

F32 = jnp.float32
BF16 = jnp.bfloat16
I32 = jnp.int32
U32 = jnp.uint32

D_MODEL = 1024
HEAD_DIM = 64
ATTN_WIDTH = 512
N_Q_HEADS = 8
N_KV_HEADS = 2
Q_PER_KV = 4
KV_WIDTH = 128
BLK = 128
GMLP_WIDTH = 512
IN_PROJ = 1792
N_GROUPS = 4
E_PER_GROUP = 8
N_EXPERTS = 32
D_EXPERT = 256
RMS_EPS = 1e-6
LN_EPS = 1e-5
NEG_INF = -1e30

LANES = 128
SUBLANES = 8
CHUNKS = D_MODEL // LANES
TM_MIX = 512
TB_ROUTE = 512
TE = 512
TB_COMB = 512
COMB_ROWS = 32
DISP_ROWS = 32
VMEM_LIMIT = 56 * 1024 * 1024

Q_OFF, K_OFF, V_OFF = 0, 512, 640
GU_OFF, GV_OFF = 768, 1280
GROUP_ROW0, EXPERT_ROW0 = 0, 8


def _rms(x, g):
    return (x * lax.rsqrt(jnp.mean(x * x, axis=-1, keepdims=True) + RMS_EPS)) * g


def _gelu(x):
    return 0.5 * x * (1.0 + lax.erf(x * F32(0.7071067811865476)))


def _store_token_tiles(ref2d, val):
    n = val.shape[0]
    for i in range(n // SUBLANES):
        for c in range(CHUNKS):
            ref2d[pl.ds(i * SUBLANES * CHUNKS + c, SUBLANES, stride=CHUNKS), :] = (
                val[i * SUBLANES:(i + 1) * SUBLANES, c * LANES:(c + 1) * LANES])


def _load_token_tiles(ref2d, n):
    cols = []
    for c in range(CHUNKS):
        cols.append(jnp.concatenate(
            [ref2d[pl.ds(i * SUBLANES * CHUNKS + c, SUBLANES, stride=CHUNKS), :] for i in range(n // SUBLANES)],
            axis=0))
    return jnp.concatenate(cols, axis=1)


def _mix_kernel(sinks_ref, x_ref, g1_ref, win_ref, wsp_ref, bsp_ref, lng_ref, lnb_ref,
                ag_ref, gg_ref, wout_ref, h_ref, proj_s, mixed_s, k4p_s, v4p_s):
    j = pl.program_id(1)
    tm = x_ref.shape[0]

    @pl.when(j == 0)
    def _():
        k4p_s[...] = jnp.zeros_like(k4p_s)
        v4p_s[...] = jnp.zeros_like(v4p_s)

    xn = _rms(x_ref[...], g1_ref[...])
    proj_s[...] = jnp.dot(xn.astype(BF16), win_ref[...], preferred_element_type=F32)

    lane128 = lax.broadcasted_iota(I32, (BLK, LANES), 1)
    lo_half = lane128 < HEAD_DIM
    lane256 = lax.broadcasted_iota(I32, (BLK, 2 * LANES), 1)
    head_masks = [(lane256 >= h * HEAD_DIM) & (lane256 < (h + 1) * HEAD_DIM) for h in range(Q_PER_KV)]
    key = lax.broadcasted_iota(I32, (2 * BLK, Q_PER_KV * BLK), 0)
    qry = lax.broadcasted_iota(I32, (2 * BLK, Q_PER_KV * BLK), 1) & (BLK - 1)
    in_window = (key > qry) & (key <= qry + BLK)
    first_ok = in_window & ((key >= BLK) | (j > 0))

    trow = lax.broadcasted_iota(I32, (BLK, 2 * BLK), 0)
    tcol = lax.broadcasted_iota(I32, (BLK, 2 * BLK), 1) & (BLK - 1)
    causal2 = tcol <= trow
    wsp = [jnp.where(causal2, wsp_ref[p], 0.0).astype(BF16) for p in range(GMLP_WIDTH // LANES)]

    scale = F32(HEAD_DIM ** -0.5)
    k4_prev = [k4p_s[kvh] for kvh in range(N_KV_HEADS)]
    v4t_prev = [v4p_s[kvh] for kvh in range(N_KV_HEADS)]
    for i in range(tm // BLK):
        r0 = i * BLK
        kk = proj_s[r0:r0 + BLK, K_OFF:K_OFF + KV_WIDTH]
        vv = proj_s[r0:r0 + BLK, V_OFF:V_OFF + KV_WIDTH]
        kk_r = pltpu.roll(kk, HEAD_DIM, 1)
        vv_r = pltpu.roll(vv, HEAD_DIM, 1)
        o_parts = []
        for kvh in range(N_KV_HEADS):
            k2 = jnp.where(lo_half, kk, kk_r) if kvh == 0 else jnp.where(lo_half, kk_r, kk)
            v2 = jnp.where(lo_half, vv, vv_r) if kvh == 0 else jnp.where(lo_half, vv_r, vv)
            k4 = jnp.concatenate([k2, k2], axis=1).astype(BF16)
            v4t = jnp.transpose(jnp.concatenate([v2, v2], axis=1)).astype(BF16)
            kband = jnp.concatenate([k4_prev[kvh], k4], axis=0)
            vband_t = jnp.concatenate([v4t_prev[kvh], v4t], axis=1)
            k4_prev[kvh] = k4
            v4t_prev[kvh] = v4t
            q4 = proj_s[r0:r0 + BLK, Q_OFF + kvh * 256:Q_OFF + (kvh + 1) * 256] * scale
            qs = jnp.concatenate([jnp.where(m, q4, 0.0) for m in head_masks], axis=0).astype(BF16)
            st = lax.dot_general(kband, qs, (((1,), (1,)), ((), ())),
                                 preferred_element_type=F32)
            st = jnp.where(first_ok if i == 0 else in_window, st, NEG_INF)
            sink = jnp.concatenate(
                [jnp.full((1, BLK), sinks_ref[kvh * Q_PER_KV + h], F32) for h in range(Q_PER_KV)], axis=1)
            m = jnp.maximum(jnp.max(st, axis=0, keepdims=True), sink)
            p = jnp.exp(st - m)
            denom = jnp.sum(p, axis=0, keepdims=True) + jnp.exp(sink - m)
            pvt = jnp.dot(vband_t, p.astype(BF16), preferred_element_type=F32)
            inv_denom = 1.0 / denom
            zt = jnp.concatenate([pvt[h * HEAD_DIM:(h + 1) * HEAD_DIM, h * BLK:(h + 1) * BLK]
                                  * inv_denom[:, h * BLK:(h + 1) * BLK] for h in range(Q_PER_KV)], axis=0)
            o_parts.append(jnp.transpose(zt))
        a = jnp.concatenate(o_parts, axis=1)
        mixed_s[r0:r0 + BLK, 0:ATTN_WIDTH] = _rms(a, ag_ref[...]).astype(BF16)

        u = _gelu(proj_s[r0:r0 + BLK, GU_OFF:GU_OFF + GMLP_WIDTH])
        vg = _gelu(proj_s[r0:r0 + BLK, GV_OFF:GV_OFF + GMLP_WIDTH])
        mu = jnp.mean(vg, axis=-1, keepdims=True)
        var = jnp.mean(jnp.square(vg - mu), axis=-1, keepdims=True)
        vln = (vg - mu) * lax.rsqrt(var + LN_EPS) * lng_ref[...] + lnb_ref[...]
        s_parts = []
        for pidx in range(GMLP_WIDTH // LANES):
            vcol = vln[:, pidx * LANES:(pidx + 1) * LANES]
            rhs = jnp.concatenate([jnp.where(lo_half, vcol, 0.0), jnp.where(lo_half, 0.0, vcol)],
                                  axis=0).astype(BF16)
            s_parts.append(jnp.dot(wsp[pidx], rhs, preferred_element_type=F32))
        sg = jnp.concatenate(s_parts, axis=1) + bsp_ref[...]
        g = u * sg
        mixed_s[r0:r0 + BLK, ATTN_WIDTH:D_MODEL] = _rms(g, gg_ref[...]).astype(BF16)

    for kvh in range(N_KV_HEADS):
        k4p_s[kvh] = k4_prev[kvh]
        v4p_s[kvh] = v4t_prev[kvh]
    h_ref[...] = x_ref[...] + jnp.dot(mixed_s[...], wout_ref[...], preferred_element_type=F32)


def _mix(x2, sinks, g1, w_in, wsp, bsp, lng, lnb, ag, gg, w_out, batch, seq):
    t = x2.shape[0]
    tm = TM_MIX
    nj = seq // tm
    full = lambda shape: pl.BlockSpec(shape, lambda b, j: (0,) * len(shape))
    return pl.pallas_call(
        _mix_kernel,
        grid=(batch, nj),
        in_specs=[
            pl.BlockSpec(memory_space=pltpu.SMEM),
            pl.BlockSpec((tm, D_MODEL), lambda b, j: (b * nj + j, 0)),
            full((1, D_MODEL)),
            full((D_MODEL, IN_PROJ)),
            full((GMLP_WIDTH // LANES, BLK, 2 * BLK)),
            full((BLK, GMLP_WIDTH)),
            full((1, GMLP_WIDTH)),
            full((1, GMLP_WIDTH)),
            full((1, ATTN_WIDTH)),
            full((1, GMLP_WIDTH)),
            full((D_MODEL, D_MODEL)),
        ],
        out_specs=pl.BlockSpec((tm, D_MODEL), lambda b, j: (b * nj + j, 0)),
        out_shape=jax.ShapeDtypeStruct((t, D_MODEL), F32),
        scratch_shapes=[
            pltpu.VMEM((tm, IN_PROJ), F32),
            pltpu.VMEM((tm, D_MODEL), BF16),
            pltpu.VMEM((N_KV_HEADS, BLK, 2 * LANES), BF16),
            pltpu.VMEM((N_KV_HEADS, 2 * LANES, BLK), BF16),
        ],
        compiler_params=pltpu.CompilerParams(
            dimension_semantics=("arbitrary", "arbitrary"), vmem_limit_bytes=VMEM_LIMIT),
        name="mix",
    )(sinks, x2, g1, w_in, wsp, bsp, lng, lnb, ag, gg, w_out)


INFO_W1, INFO_W2, INFO_E1, INFO_E2, INFO_R1, INFO_R2 = range(6)


def _route_kernel(h_ref, g_ref, whl_ref, b_ref, info_ref, wts_ref, cnt_ref, carry_s):
    i = pl.program_id(0)
    tb = h_ref.shape[0]

    @pl.when(i == 0)
    def _():
        carry_s[...] = jnp.zeros_like(carry_s)

    hn = _rms(h_ref[...], g_ref[...])
    hi = hn.astype(BF16)
    lo = (hn - hi.astype(F32)).astype(BF16)
    hi_both = jnp.dot(hi, whl_ref[...], preferred_element_type=F32)
    logits = (hi_both[:, 0:LANES]
              + (hi_both[:, LANES:2 * LANES] + jnp.dot(lo, whl_ref[:, 0:LANES], preferred_element_type=F32))
              + b_ref[...])
    lt = jnp.transpose(logits)
    row8 = lax.broadcasted_iota(I32, (SUBLANES, tb), 0)
    ninf = F32(-jnp.inf)

    def top_row(vals):
        best = jnp.max(vals, axis=0, keepdims=True)
        return best, jnp.min(jnp.where(vals == best, row8, SUBLANES), axis=0, keepdims=True)

    gl = jnp.where(row8 < N_GROUPS, lt[GROUP_ROW0:GROUP_ROW0 + SUBLANES], ninf)
    gmax, gidx = top_row(gl)
    g_w = 1.0 / jnp.sum(jnp.exp(gl - gmax), axis=0, keepdims=True)

    el = lt[EXPERT_ROW0:EXPERT_ROW0 + E_PER_GROUP]
    for g in range(1, N_GROUPS):
        el = jnp.where(gidx == g, lt[EXPERT_ROW0 + g * E_PER_GROUP:EXPERT_ROW0 + (g + 1) * E_PER_GROUP], el)
    v1, i1 = top_row(el)
    v2, i2 = top_row(jnp.where(row8 == i1, ninf, el))
    t2 = jnp.exp(v2 - v1)
    inv = 1.0 / (1.0 + t2)
    w1 = inv * g_w
    w2 = (t2 * inv) * g_w
    e1 = gidx * E_PER_GROUP + i1
    e2 = gidx * E_PER_GROUP + i2

    rowe = lax.broadcasted_iota(I32, (N_EXPERTS, tb), 0)
    sel1 = rowe == e1
    sel2 = rowe == e2
    oh = jnp.where(sel1 | sel2, 1.0, 0.0)
    rr = lax.broadcasted_iota(I32, (tb, tb), 0)
    cc = lax.broadcasted_iota(I32, (tb, tb), 1)
    triu = jnp.where(rr <= cc, 1.0, 0.0).astype(BF16)
    incl = jnp.dot(oh.astype(BF16), triu, preferred_element_type=F32)
    carry = carry_s[...]
    before = incl - oh + carry[:, 0:1]
    rank1 = jnp.sum(jnp.where(sel1, before, 0.0), axis=0, keepdims=True)
    rank2 = jnp.sum(jnp.where(sel2, before, 0.0), axis=0, keepdims=True)
    carry = carry + incl[:, tb - 1:tb]
    carry_s[...] = carry
    cnt_ref[...] = carry

    rows = (w1, w2, e1.astype(F32), e2.astype(F32), rank1, rank2)
    info = jnp.zeros((SUBLANES, tb), F32)
    for k, val in enumerate(rows):
        info = jnp.where(row8 == k, val, info)
    info_ref[0] = info
    row128 = lax.broadcasted_iota(I32, (LANES, tb), 0)
    wts_ref[...] = jnp.transpose(jnp.where(row128 == INFO_W1, w1, jnp.where(row128 == INFO_W2, w2, 0.0)))


def _route(h, g, whl, bias):
    t = h.shape[0]
    tb = TB_ROUTE
    nsteps = t // tb
    full = lambda shape: pl.BlockSpec(shape, lambda i: (0,) * len(shape))
    return pl.pallas_call(
        _route_kernel,
        grid=(nsteps,),
        in_specs=[
            pl.BlockSpec((tb, D_MODEL), lambda i: (i, 0)),
            full((1, D_MODEL)),
            full((D_MODEL, 2 * LANES)),
            full((1, LANES)),
        ],
        out_specs=[
            pl.BlockSpec((1, SUBLANES, tb), lambda i: (i, 0, 0)),
            pl.BlockSpec((tb, LANES), lambda i: (i, 0)),
            full((N_EXPERTS, LANES)),
        ],
        out_shape=[
            jax.ShapeDtypeStruct((nsteps, SUBLANES, tb), F32),
            jax.ShapeDtypeStruct((t, LANES), F32),
            jax.ShapeDtypeStruct((N_EXPERTS, LANES), F32),
        ],
        scratch_shapes=[pltpu.VMEM((N_EXPERTS, LANES), F32)],
        compiler_params=pltpu.CompilerParams(
            dimension_semantics=("arbitrary",), vmem_limit_bytes=VMEM_LIMIT),
        name="route",
    )(h, g, whl, bias)


def _dispatch_kernel(cnt_ref, off_ref, nt_ref, h_ref, g_ref, info_ref, offcol_ref, xs_hbm, dst_ref,
                     rows_s, zeros_s, dstv_s, dsts_s, sems, fill_sem, *, n_tiles):
    i = pl.program_id(0)
    n_steps = pl.num_programs(0)
    tb = h_ref.shape[0]
    slot = i % 2

    def row_copy(buf, t, dst_row):
        return pltpu.make_async_copy(rows_s.at[buf, pl.ds(t * CHUNKS, CHUNKS), :], xs_hbm.at[dst_row],
                                     sems.at[buf])

    def drain_tile(buf):
        for _ in range(2 * tb):
            row_copy(buf, 0, 0).wait()

    @pl.when(i >= 2)
    def _():
        drain_tile(slot)

    info = info_ref[0]
    rowe = lax.broadcasted_iota(I32, (N_EXPERTS, tb), 0).astype(F32)
    off = offcol_ref[...][:, 0:1]
    dst_rows = []
    for ke, kr in ((INFO_E1, INFO_R1), (INFO_E2, INFO_R2)):
        base = jnp.sum(jnp.where(rowe == info[ke:ke + 1, :], off, 0.0), axis=0, keepdims=True)
        dst_rows.append(base + info[kr:kr + 1, :])
    row8 = lax.broadcasted_iota(I32, (SUBLANES, tb), 0)
    dst_t = jnp.where(row8 == 0, dst_rows[0], jnp.where(row8 == 1, dst_rows[1], 0.0)).astype(I32)
    dstv_s[...] = dst_t
    dst_ref[0] = dst_t
    pltpu.sync_copy(dstv_s, dsts_s)

    for r0 in range(0, tb, DISP_ROWS):
        _store_token_tiles(rows_s.at[slot, pl.ds(r0 * CHUNKS, DISP_ROWS * CHUNKS), :],
                           _rms(h_ref[r0:r0 + DISP_ROWS, :], g_ref[...]))
        for t in range(r0, r0 + DISP_ROWS):
            row_copy(slot, t, dsts_s[0, t]).start(priority=0)
            row_copy(slot, t, dsts_s[1, t]).start(priority=1)

    @pl.when(i == n_steps - 1)
    def _():
        zeros_s[...] = jnp.zeros_like(zeros_s)
        half = TE // 2

        def fill_copy(pos, size):
            return pltpu.make_async_copy(zeros_s.at[pl.ds(0, size)], xs_hbm.at[pl.ds(pos, size)], fill_sem)

        def for_each_gap(action):
            for e in range(N_EXPERTS):
                n = cnt_ref[e]
                gap = (-n) & (TE - 1)
                pos = off_ref[e] + n
                size = half
                while size >= 1:
                    @pl.when((gap & size) != 0)
                    def _(pos=pos, size=size):
                        action(fill_copy(pos, size))
                    pos = pos + (gap & size)
                    size //= 2

        def tail_issue(k, c):
            fill_copy(k * half, half).start()
            return c

        def tail_wait(k, c):
            fill_copy(0, half).wait()
            return c

        for_each_gap(lambda cp: cp.start())
        lax.fori_loop(2 * nt_ref[0], 2 * n_tiles, tail_issue, 0)
        for_each_gap(lambda cp: cp.wait())
        lax.fori_loop(2 * nt_ref[0], 2 * n_tiles, tail_wait, 0)
        drain_tile(slot)

        @pl.when(i >= 1)
        def _():
            drain_tile(1 - slot)


def _dispatch(counts, offsets, nt, h, g, info, off_col, n_tiles):
    t = h.shape[0]
    tb = TB_ROUTE
    nsteps = t // tb
    full = lambda shape: pl.BlockSpec(shape, lambda i, *_: (0,) * len(shape))
    return pl.pallas_call(
        functools.partial(_dispatch_kernel, n_tiles=n_tiles),
        grid_spec=pltpu.PrefetchScalarGridSpec(
            num_scalar_prefetch=3,
            grid=(nsteps,),
            in_specs=[
                pl.BlockSpec((tb, D_MODEL), lambda i, *_: (i, 0)),
                full((1, D_MODEL)),
                pl.BlockSpec((1, SUBLANES, tb), lambda i, *_: (i, 0, 0)),
                full((N_EXPERTS, LANES)),
            ],
            out_specs=[
                pl.BlockSpec(memory_space=pl.ANY),
                pl.BlockSpec((1, 8, tb), lambda i, *_: (i, 0, 0)),
            ],
            scratch_shapes=[
                pltpu.VMEM((2, tb * CHUNKS, LANES), F32),
                pltpu.VMEM((TE // 2, CHUNKS, LANES), F32),
                pltpu.VMEM((8, tb), I32),
                pltpu.SMEM((8, tb), I32),
                pltpu.SemaphoreType.DMA((2,)),
                pltpu.SemaphoreType.DMA,
            ],
        ),
        out_shape=[
            jax.ShapeDtypeStruct((n_tiles * TE, CHUNKS, LANES), F32),
            jax.ShapeDtypeStruct((nsteps, 8, tb), I32),
        ],
        compiler_params=pltpu.CompilerParams(
            dimension_semantics=("arbitrary",), vmem_limit_bytes=VMEM_LIMIT),
        name="dispatch",
    )(counts, offsets, nt, h, g, info, off_col)


def _expert_kernel(te_ref, nt_ref, x_ref, wg_ref, wu_ref, wd_ref, y_ref, wgu_s, wdb_s):
    i = pl.program_id(0)

    @pl.when(i >= nt_ref[0])
    def _():
        y_ref[...] = jnp.zeros_like(y_ref)

    @pl.when(i < nt_ref[0])
    def _():
        prev = te_ref[jnp.maximum(i - 1, 0)]

        @pl.when((i == 0) | (te_ref[i] != prev))
        def _():
            wgu_s[:, 0:D_EXPERT] = wg_ref[0].astype(BF16)
            wgu_s[:, D_EXPERT:2 * D_EXPERT] = wu_ref[0].astype(BF16)
            wdb_s[...] = wd_ref[0].astype(BF16)

        x = _load_token_tiles(x_ref, TE).astype(BF16)
        ab = jnp.dot(x, wgu_s[...], preferred_element_type=F32)
        a = ab[:, 0:D_EXPERT]
        b = ab[:, D_EXPERT:2 * D_EXPERT]
        hid = (a * jax.nn.sigmoid(a)) * b
        y = jnp.dot(hid.astype(BF16), wdb_s[...], preferred_element_type=F32)
        _store_token_tiles(y_ref, y)


def _experts(te, nt, xs, w_gate, w_up, w_down, n_tiles):
    tile = lambda i, te, nt: (i, 0)
    busy_tile = lambda i, te, nt: (jnp.minimum(i, nt[0] - 1), 0)
    wblk = lambda i, te, nt: (te[i], 0, 0)
    return pl.pallas_call(
        _expert_kernel,
        grid_spec=pltpu.PrefetchScalarGridSpec(
            num_scalar_prefetch=2,
            grid=(n_tiles,),
            in_specs=[
                pl.BlockSpec((TE * CHUNKS, LANES), busy_tile),
                pl.BlockSpec((1, D_MODEL, D_EXPERT), wblk),
                pl.BlockSpec((1, D_MODEL, D_EXPERT), wblk),
                pl.BlockSpec((1, D_EXPERT, D_MODEL), wblk),
            ],
            out_specs=pl.BlockSpec((TE * CHUNKS, LANES), tile),
            scratch_shapes=[
                pltpu.VMEM((D_MODEL, 2 * D_EXPERT), BF16),
                pltpu.VMEM((D_EXPERT, D_MODEL), BF16),
            ],
        ),
        out_shape=jax.ShapeDtypeStruct(xs.shape, F32),
        compiler_params=pltpu.CompilerParams(
            dimension_semantics=("arbitrary",), vmem_limit_bytes=VMEM_LIMIT),
        name="experts",
    )(te, nt, xs, w_gate, w_up, w_down)


def _combine_kernel(dst0_ref, dstn_ref, h_ref, wts_ref, g_ref, ys_hbm, out_ref, dsts_s, ybuf_s, sem):
    i = pl.program_id(0)
    n_steps = pl.num_programs(0)
    tb = h_ref.shape[0]
    slot = i % 2

    def row_copy(buf, k, t, src_row):
        return pltpu.make_async_copy(ys_hbm.at[src_row], ybuf_s.at[buf, k, pl.ds(t * CHUNKS, CHUNKS), :], sem)

    def issue_tile(buf):
        for t in range(tb):
            row_copy(buf, 0, t, dsts_s[0, t]).start(priority=0)
            row_copy(buf, 1, t, dsts_s[1, t]).start(priority=1)

    def drain_tile():
        for _ in range(2 * tb):
            row_copy(0, 0, 0, 0).wait()

    @pl.when(i == 0)
    def _():
        pltpu.sync_copy(dst0_ref.at[0], dsts_s)
        issue_tile(0)

    pltpu.sync_copy(dstn_ref.at[0], dsts_s)
    drain_tile()

    for r0 in range(0, tb, COMB_ROWS):
        rows = slice(r0, r0 + COMB_ROWS)
        part = [_load_token_tiles(ybuf_s.at[slot, k, pl.ds(r0 * CHUNKS, COMB_ROWS * CHUNKS), :], COMB_ROWS)
                for k in range(2)]
        w = wts_ref[rows, :]
        moe = w[:, INFO_W1:INFO_W1 + 1] * part[0] + w[:, INFO_W2:INFO_W2 + 1] * part[1]
        out_ref[rows, :] = _rms(h_ref[rows, :] + moe, g_ref[...])
        for t in range(r0, r0 + COMB_ROWS):
            row_copy(1 - slot, 0, t, dsts_s[0, t]).start(priority=0)
            row_copy(1 - slot, 1, t, dsts_s[1, t]).start(priority=1)

    @pl.when(i == n_steps - 1)
    def _():
        drain_tile()


def _combine(dst, h, wts, g, ys):
    t = h.shape[0]
    tb = TB_COMB
    n_steps = t // tb
    return pl.pallas_call(
        _combine_kernel,
        grid=(n_steps,),
        in_specs=[
            pl.BlockSpec((1, 8, tb), lambda i: (0, 0, 0)),
            pl.BlockSpec((1, 8, tb), lambda i: (jnp.minimum(i + 1, n_steps - 1), 0, 0)),
            pl.BlockSpec((tb, D_MODEL), lambda i: (i, 0)),
            pl.BlockSpec((tb, LANES), lambda i: (i, 0)),
            pl.BlockSpec((1, D_MODEL), lambda i: (0, 0)),
            pl.BlockSpec(memory_space=pl.ANY),
        ],
        out_specs=pl.BlockSpec((tb, D_MODEL), lambda i: (i, 0)),
        out_shape=jax.ShapeDtypeStruct((t, D_MODEL), F32),
        scratch_shapes=[
            pltpu.SMEM((8, tb), I32),
            pltpu.VMEM((2, 2, tb * CHUNKS, LANES), F32),
            pltpu.SemaphoreType.DMA,
        ],
        compiler_params=pltpu.CompilerParams(
            dimension_semantics=("arbitrary",), vmem_limit_bytes=VMEM_LIMIT),
        name="combine",
    )(dst, dst, h, wts, g, ys)


def _tile_tables(counts, n_tiles):
    per = (counts + (TE - 1)) // TE
    cum = jnp.cumsum(per)
    total = cum[-1]
    offsets = ((cum - per) * TE).astype(I32)
    idx = jnp.minimum(jnp.arange(n_tiles, dtype=I32), total - 1)
    te = jnp.minimum(jnp.sum((idx[:, None] >= cum[None, :]).astype(I32), axis=1), N_EXPERTS - 1)
    return offsets, te, total.reshape(1).astype(I32)


def kernel(x, mix_norm_g, w_in, attn_sinks, w_spatial, b_spatial, gmlp_ln_g, gmlp_ln_b, attn_out_g, gmlp_out_g, w_out, ffn_norm_g, w_group_router, b_group_router, w_expert_router, b_expert_router, w_gate, w_up, w_down, final_norm_g):
    batch, seq, _ = x.shape
    t = batch * seq
    depth = mix_norm_g.shape[0]
    assert depth == 1, "the final norm is fused into the last layer's combine kernel"
    n_tiles = (2 * t) // TE + N_EXPERTS
    h = x.reshape(t, D_MODEL)
    for l in range(depth):
        n_pairs = GMLP_WIDTH // LANES
        wsp = w_spatial[l].reshape(n_pairs, 2, BLK, BLK).transpose(0, 2, 1, 3).reshape(n_pairs, BLK, 2 * BLK)
        bsp = jnp.repeat(b_spatial[l].T, HEAD_DIM, axis=1)
        h = _mix(h, attn_sinks[l], mix_norm_g[l][None], w_in[l].astype(BF16), wsp, bsp,
                 gmlp_ln_g[l][None], gmlp_ln_b[l][None], attn_out_g[l][None], gmlp_out_g[l][None],
                 w_out[l].astype(BF16), batch, seq)

        w_r = jnp.concatenate(
            [w_group_router[l], jnp.zeros((D_MODEL, EXPERT_ROW0 - N_GROUPS), F32),
             w_expert_router[l].transpose(1, 0, 2).reshape(D_MODEL, N_EXPERTS)], axis=1)
        w_r = jnp.pad(w_r, ((0, 0), (0, LANES - w_r.shape[1])))
        b_r = jnp.concatenate([b_group_router[l], jnp.zeros((EXPERT_ROW0 - N_GROUPS,), F32),
                               b_expert_router[l].reshape(-1)])
        b_r = jnp.pad(b_r, (0, LANES - b_r.shape[0]))[None]
        w_r_top = lax.bitcast_convert_type(
            lax.bitcast_convert_type(w_r, U32) & U32(0xFFFF0000), F32)
        w_r_hl = jnp.concatenate([w_r_top.astype(BF16), (w_r - w_r_top).astype(BF16)], axis=1)
        info, wts, cnt = _route(h, ffn_norm_g[l][None], w_r_hl, b_r)
        counts = cnt[:, 0].astype(I32)
        offsets, te, nt = _tile_tables(counts, n_tiles)
        off_col = jnp.broadcast_to(offsets.astype(F32)[:, None], (N_EXPERTS, LANES))
        xs, dst = _dispatch(counts, offsets, nt, h, ffn_norm_g[l][None], info, off_col, n_tiles)
        ys = _experts(te, nt, xs.reshape(-1, LANES), w_gate[l], w_up[l], w_down[l], n_tiles)
        h = _combine(dst, h, wts, final_norm_g[None], ys.reshape(xs.shape))
    return h.reshape(batch, seq, D_MODEL)
```

```python
import functools

import jax
import jax.numpy as jnp
from jax import lax
from jax.experimental import pallas as pl
from jax.experimental.pallas import tpu as pltpu

F32 = jnp.float32
BF16 = jnp.bfloat16
I32 = jnp.int32
U32 = jnp.uint32

D_MODEL = 1024
HEAD_DIM = 64
ATTN_WIDTH = 512
N_Q_HEADS = 8
N_KV_HEADS = 2
Q_PER_KV = 4
KV_WIDTH = 128
BLK = 128
GMLP_WIDTH = 512
IN_PROJ = 1792
N_GROUPS = 4
E_PER_GROUP = 8
N_EXPERTS = 32
D_EXPERT = 256
RMS_EPS = 1e-6
LN_EPS = 1e-5
NEG_INF = -1e30

LANES = 128
SUBLANES = 8
CHUNKS = D_MODEL // LANES
TM_MIX = 512
TB_ROUTE = 512
TE = 512
X_RING = 3
TB_COMB = 512
COMB_ROWS = 32
DISP_ROWS = 32
VMEM_LIMIT = 56 * 1024 * 1024

Q_OFF, K_OFF, V_OFF = 0, 512, 640
GU_OFF, GV_OFF = 768, 1280
GROUP_ROW0, EXPERT_ROW0 = 0, 8


def _rms(x, g):
    return (x * lax.rsqrt(jnp.mean(x * x, axis=-1, keepdims=True) + RMS_EPS)) * g


def _gelu(x):
    return 0.5 * x * (1.0 + lax.erf(x * F32(0.7071067811865476)))


def _store_token_tiles(ref2d, val):
    n = val.shape[0]
    for i in range(n // SUBLANES):
        for c in range(CHUNKS):
            ref2d[pl.ds(i * SUBLANES * CHUNKS + c, SUBLANES, stride=CHUNKS), :] = (
                val[i * SUBLANES:(i + 1) * SUBLANES, c * LANES:(c + 1) * LANES])


def _load_token_tiles(ref2d, n):
    cols = []
    for c in range(CHUNKS):
        cols.append(jnp.concatenate(
            [ref2d[pl.ds(i * SUBLANES * CHUNKS + c, SUBLANES, stride=CHUNKS), :] for i in range(n // SUBLANES)],
            axis=0))
    return jnp.concatenate(cols, axis=1)


def _mix_kernel(sinks_ref, x_ref, g1_ref, win_ref, wsp_ref, bsp_ref, lng_ref, lnb_ref,
                ag_ref, gg_ref, wout_ref, h_ref, proj_s, mixed_s, k4p_s, v4p_s):
    j = pl.program_id(1)
    tm = x_ref.shape[0]

    @pl.when(j == 0)
    def _():
        k4p_s[...] = jnp.zeros_like(k4p_s)
        v4p_s[...] = jnp.zeros_like(v4p_s)

    xn = _rms(x_ref[...], g1_ref[...])
    proj_s[...] = jnp.dot(xn.astype(BF16), win_ref[...], preferred_element_type=F32)

    lane128 = lax.broadcasted_iota(I32, (BLK, LANES), 1)
    lo_half = lane128 < HEAD_DIM
    lane256 = lax.broadcasted_iota(I32, (BLK, 2 * LANES), 1)
    head_masks = [(lane256 >= h * HEAD_DIM) & (lane256 < (h + 1) * HEAD_DIM) for h in range(Q_PER_KV)]
    key = lax.broadcasted_iota(I32, (2 * BLK, Q_PER_KV * BLK), 0)
    qry = lax.broadcasted_iota(I32, (2 * BLK, Q_PER_KV * BLK), 1) & (BLK - 1)
    in_window = (key > qry) & (key <= qry + BLK)
    first_ok = in_window & ((key >= BLK) | (j > 0))

    trow = lax.broadcasted_iota(I32, (BLK, 2 * BLK), 0)
    tcol = lax.broadcasted_iota(I32, (BLK, 2 * BLK), 1) & (BLK - 1)
    causal2 = tcol <= trow
    wsp = [jnp.where(causal2, wsp_ref[p], 0.0).astype(BF16) for p in range(GMLP_WIDTH // LANES)]

    scale = F32(HEAD_DIM ** -0.5)
    k4_prev = [k4p_s[kvh] for kvh in range(N_KV_HEADS)]
    v4t_prev = [v4p_s[kvh] for kvh in range(N_KV_HEADS)]
    for i in range(tm // BLK):
        r0 = i * BLK
        kk = proj_s[r0:r0 + BLK, K_OFF:K_OFF + KV_WIDTH]
        vv = proj_s[r0:r0 + BLK, V_OFF:V_OFF + KV_WIDTH]
        kk_r = pltpu.roll(kk, HEAD_DIM, 1)
        vv_r = pltpu.roll(vv, HEAD_DIM, 1)
        o_parts = []
        for kvh in range(N_KV_HEADS):
            k2 = jnp.where(lo_half, kk, kk_r) if kvh == 0 else jnp.where(lo_half, kk_r, kk)
            v2 = jnp.where(lo_half, vv, vv_r) if kvh == 0 else jnp.where(lo_half, vv_r, vv)
            k4 = jnp.concatenate([k2, k2], axis=1).astype(BF16)
            v4t = jnp.transpose(jnp.concatenate([v2, v2], axis=1)).astype(BF16)
            kband = jnp.concatenate([k4_prev[kvh], k4], axis=0)
            vband_t = jnp.concatenate([v4t_prev[kvh], v4t], axis=1)
            k4_prev[kvh] = k4
            v4t_prev[kvh] = v4t
            q4 = proj_s[r0:r0 + BLK, Q_OFF + kvh * 256:Q_OFF + (kvh + 1) * 256] * scale
            qs = jnp.concatenate([jnp.where(m, q4, 0.0) for m in head_masks], axis=0).astype(BF16)
            st = lax.dot_general(kband, qs, (((1,), (1,)), ((), ())),
                                 preferred_element_type=F32)
            st = jnp.where(first_ok if i == 0 else in_window, st, NEG_INF)
            sink = jnp.concatenate(
                [jnp.full((1, BLK), sinks_ref[kvh * Q_PER_KV + h], F32) for h in range(Q_PER_KV)], axis=1)
            m = jnp.maximum(jnp.max(st, axis=0, keepdims=True), sink)
            p = jnp.exp(st - m)
            denom = jnp.sum(p, axis=0, keepdims=True) + jnp.exp(sink - m)
            pvt = jnp.dot(vband_t, p.astype(BF16), preferred_element_type=F32)
            inv_denom = 1.0 / denom
            zt = jnp.concatenate([pvt[h * HEAD_DIM:(h + 1) * HEAD_DIM, h * BLK:(h + 1) * BLK]
                                  * inv_denom[:, h * BLK:(h + 1) * BLK] for h in range(Q_PER_KV)], axis=0)
            o_parts.append(jnp.transpose(zt))
        a = jnp.concatenate(o_parts, axis=1)
        mixed_s[r0:r0 + BLK, 0:ATTN_WIDTH] = _rms(a, ag_ref[...]).astype(BF16)

        u = _gelu(proj_s[r0:r0 + BLK, GU_OFF:GU_OFF + GMLP_WIDTH])
        vg = _gelu(proj_s[r0:r0 + BLK, GV_OFF:GV_OFF + GMLP_WIDTH])
        mu = jnp.mean(vg, axis=-1, keepdims=True)
        var = jnp.mean(jnp.square(vg - mu), axis=-1, keepdims=True)
        vln = (vg - mu) * lax.rsqrt(var + LN_EPS) * lng_ref[...] + lnb_ref[...]
        s_parts = []
        for pidx in range(GMLP_WIDTH // LANES):
            vcol = vln[:, pidx * LANES:(pidx + 1) * LANES]
            rhs = jnp.concatenate([jnp.where(lo_half, vcol, 0.0), jnp.where(lo_half, 0.0, vcol)],
                                  axis=0).astype(BF16)
            s_parts.append(jnp.dot(wsp[pidx], rhs, preferred_element_type=F32))
        sg = jnp.concatenate(s_parts, axis=1) + bsp_ref[...]
        g = u * sg
        mixed_s[r0:r0 + BLK, ATTN_WIDTH:D_MODEL] = _rms(g, gg_ref[...]).astype(BF16)

    for kvh in range(N_KV_HEADS):
        k4p_s[kvh] = k4_prev[kvh]
        v4p_s[kvh] = v4t_prev[kvh]
    h_ref[...] = x_ref[...] + jnp.dot(mixed_s[...], wout_ref[...], preferred_element_type=F32)


def _mix(x2, sinks, g1, w_in, wsp, bsp, lng, lnb, ag, gg, w_out, batch, seq):
    t = x2.shape[0]
    tm = TM_MIX
    nj = seq // tm
    full = lambda shape: pl.BlockSpec(shape, lambda b, j: (0,) * len(shape))
    return pl.pallas_call(
        _mix_kernel,
        grid=(batch, nj),
        in_specs=[
            pl.BlockSpec(memory_space=pltpu.SMEM),
            pl.BlockSpec((tm, D_MODEL), lambda b, j: (b * nj + j, 0)),
            full((1, D_MODEL)),
            full((D_MODEL, IN_PROJ)),
            full((GMLP_WIDTH // LANES, BLK, 2 * BLK)),
            full((BLK, GMLP_WIDTH)),
            full((1, GMLP_WIDTH)),
            full((1, GMLP_WIDTH)),
            full((1, ATTN_WIDTH)),
            full((1, GMLP_WIDTH)),
            full((D_MODEL, D_MODEL)),
        ],
        out_specs=pl.BlockSpec((tm, D_MODEL), lambda b, j: (b * nj + j, 0)),
        out_shape=jax.ShapeDtypeStruct((t, D_MODEL), F32),
        scratch_shapes=[
            pltpu.VMEM((tm, IN_PROJ), F32),
            pltpu.VMEM((tm, D_MODEL), BF16),
            pltpu.VMEM((N_KV_HEADS, BLK, 2 * LANES), BF16),
            pltpu.VMEM((N_KV_HEADS, 2 * LANES, BLK), BF16),
        ],
        compiler_params=pltpu.CompilerParams(
            dimension_semantics=("arbitrary", "arbitrary"), vmem_limit_bytes=VMEM_LIMIT),
        name="mix",
    )(sinks, x2, g1, w_in, wsp, bsp, lng, lnb, ag, gg, w_out)


INFO_W1, INFO_W2, INFO_E1, INFO_E2, INFO_R1, INFO_R2 = range(6)


def _route_kernel(h_ref, g_ref, whl_ref, b_ref, info_ref, wts_ref, cnt_ref, carry_s):
    i = pl.program_id(0)
    tb = h_ref.shape[0]

    @pl.when(i == 0)
    def _():
        carry_s[...] = jnp.zeros_like(carry_s)

    hn = _rms(h_ref[...], g_ref[...])
    hi = hn.astype(BF16)
    lo = (hn - hi.astype(F32)).astype(BF16)
    hi_both = jnp.dot(hi, whl_ref[...], preferred_element_type=F32)
    logits = (hi_both[:, 0:LANES]
              + (hi_both[:, LANES:2 * LANES] + jnp.dot(lo, whl_ref[:, 0:LANES], preferred_element_type=F32))
              + b_ref[...])
    lt = jnp.transpose(logits)
    row8 = lax.broadcasted_iota(I32, (SUBLANES, tb), 0)
    ninf = F32(-jnp.inf)

    def top_row(vals):
        best = jnp.max(vals, axis=0, keepdims=True)
        return best, jnp.min(jnp.where(vals == best, row8, SUBLANES), axis=0, keepdims=True)

    gl = jnp.where(row8 < N_GROUPS, lt[GROUP_ROW0:GROUP_ROW0 + SUBLANES], ninf)
    gmax, gidx = top_row(gl)
    g_w = 1.0 / jnp.sum(jnp.exp(gl - gmax), axis=0, keepdims=True)

    el = lt[EXPERT_ROW0:EXPERT_ROW0 + E_PER_GROUP]
    for g in range(1, N_GROUPS):
        el = jnp.where(gidx == g, lt[EXPERT_ROW0 + g * E_PER_GROUP:EXPERT_ROW0 + (g + 1) * E_PER_GROUP], el)
    v1, i1 = top_row(el)
    v2, i2 = top_row(jnp.where(row8 == i1, ninf, el))
    t2 = jnp.exp(v2 - v1)
    inv = 1.0 / (1.0 + t2)
    w1 = inv * g_w
    w2 = (t2 * inv) * g_w
    e1 = gidx * E_PER_GROUP + i1
    e2 = gidx * E_PER_GROUP + i2

    rowe = lax.broadcasted_iota(I32, (N_EXPERTS, tb), 0)
    sel1 = rowe == e1
    sel2 = rowe == e2
    oh = jnp.where(sel1 | sel2, 1.0, 0.0)
    rr = lax.broadcasted_iota(I32, (tb, tb), 0)
    cc = lax.broadcasted_iota(I32, (tb, tb), 1)
    triu = jnp.where(rr <= cc, 1.0, 0.0).astype(BF16)
    incl = jnp.dot(oh.astype(BF16), triu, preferred_element_type=F32)
    carry = carry_s[...]
    before = incl - oh + carry[:, 0:1]
    rank1 = jnp.sum(jnp.where(sel1, before, 0.0), axis=0, keepdims=True)
    rank2 = jnp.sum(jnp.where(sel2, before, 0.0), axis=0, keepdims=True)
    carry = carry + incl[:, tb - 1:tb]
    carry_s[...] = carry
    cnt_ref[...] = carry

    rows = (w1, w2, e1.astype(F32), e2.astype(F32), rank1, rank2)
    info = jnp.zeros((SUBLANES, tb), F32)
    for k, val in enumerate(rows):
        info = jnp.where(row8 == k, val, info)
    info_ref[0] = info
    row128 = lax.broadcasted_iota(I32, (LANES, tb), 0)
    wts_ref[...] = jnp.transpose(jnp.where(row128 == INFO_W1, w1, jnp.where(row128 == INFO_W2, w2, 0.0)))


def _route(h, g, whl, bias):
    t = h.shape[0]
    tb = TB_ROUTE
    nsteps = t // tb
    full = lambda shape: pl.BlockSpec(shape, lambda i: (0,) * len(shape))
    return pl.pallas_call(
        _route_kernel,
        grid=(nsteps,),
        in_specs=[
            pl.BlockSpec((tb, D_MODEL), lambda i: (i, 0)),
            full((1, D_MODEL)),
            full((D_MODEL, 2 * LANES)),
            full((1, LANES)),
        ],
        out_specs=[
            pl.BlockSpec((1, SUBLANES, tb), lambda i: (i, 0, 0)),
            pl.BlockSpec((tb, LANES), lambda i: (i, 0)),
            full((N_EXPERTS, LANES)),
        ],
        out_shape=[
            jax.ShapeDtypeStruct((nsteps, SUBLANES, tb), F32),
            jax.ShapeDtypeStruct((t, LANES), F32),
            jax.ShapeDtypeStruct((N_EXPERTS, LANES), F32),
        ],
        scratch_shapes=[pltpu.VMEM((N_EXPERTS, LANES), F32)],
        compiler_params=pltpu.CompilerParams(
            dimension_semantics=("arbitrary",), vmem_limit_bytes=VMEM_LIMIT),
        name="route",
    )(h, g, whl, bias)


def _dispatch_kernel(cnt_ref, off_ref, nt_ref, h_ref, g_ref, info_ref, offcol_ref, xs_hbm, dst_ref,
                     rows_s, zeros_s, dstv_s, dsts_s, sems, fill_sem, *, n_tiles):
    i = pl.program_id(0)
    n_steps = pl.num_programs(0)
    tb = h_ref.shape[0]
    slot = i % 2

    def row_copy(buf, t, dst_row):
        return pltpu.make_async_copy(rows_s.at[buf, pl.ds(t * CHUNKS, CHUNKS), :], xs_hbm.at[dst_row],
                                     sems.at[buf])

    def drain_tile(buf):
        for _ in range(2 * tb):
            row_copy(buf, 0, 0).wait()

    @pl.when(i >= 2)
    def _():
        drain_tile(slot)

    info = info_ref[0]
    rowe = lax.broadcasted_iota(I32, (N_EXPERTS, tb), 0).astype(F32)
    off = offcol_ref[...][:, 0:1]
    dst_rows = []
    for ke, kr in ((INFO_E1, INFO_R1), (INFO_E2, INFO_R2)):
        base = jnp.sum(jnp.where(rowe == info[ke:ke + 1, :], off, 0.0), axis=0, keepdims=True)
        dst_rows.append(base + info[kr:kr + 1, :])
    row8 = lax.broadcasted_iota(I32, (SUBLANES, tb), 0)
    dst_t = jnp.where(row8 == 0, dst_rows[0], jnp.where(row8 == 1, dst_rows[1], 0.0)).astype(I32)
    dstv_s[...] = dst_t
    dst_ref[0] = dst_t
    pltpu.sync_copy(dstv_s, dsts_s)

    for r0 in range(0, tb, DISP_ROWS):
        _store_token_tiles(rows_s.at[slot, pl.ds(r0 * CHUNKS, DISP_ROWS * CHUNKS), :],
                           _rms(h_ref[r0:r0 + DISP_ROWS, :], g_ref[...]))
        for t in range(r0, r0 + DISP_ROWS):
            row_copy(slot, t, dsts_s[0, t]).start(priority=0)
            row_copy(slot, t, dsts_s[1, t]).start(priority=1)

    @pl.when(i == n_steps - 1)
    def _():
        zeros_s[...] = jnp.zeros_like(zeros_s)
        half = TE // 2

        def fill_copy(pos, size):
            return pltpu.make_async_copy(zeros_s.at[pl.ds(0, size)], xs_hbm.at[pl.ds(pos, size)], fill_sem)

        def for_each_gap(action):
            for e in range(N_EXPERTS):
                n = cnt_ref[e]
                gap = (-n) & (TE - 1)
                pos = off_ref[e] + n
                size = half
                while size >= 1:
                    @pl.when((gap & size) != 0)
                    def _(pos=pos, size=size):
                        action(fill_copy(pos, size))
                    pos = pos + (gap & size)
                    size //= 2

        def tail_issue(k, c):
            fill_copy(k * half, half).start()
            return c

        def tail_wait(k, c):
            fill_copy(0, half).wait()
            return c

        for_each_gap(lambda cp: cp.start())
        lax.fori_loop(2 * nt_ref[0], 2 * n_tiles, tail_issue, 0)
        for_each_gap(lambda cp: cp.wait())
        lax.fori_loop(2 * nt_ref[0], 2 * n_tiles, tail_wait, 0)
        drain_tile(slot)

        @pl.when(i >= 1)
        def _():
            drain_tile(1 - slot)


def _dispatch(counts, offsets, nt, h, g, info, off_col, n_tiles):
    t = h.shape[0]
    tb = TB_ROUTE
    nsteps = t // tb
    full = lambda shape: pl.BlockSpec(shape, lambda i, *_: (0,) * len(shape))
    return pl.pallas_call(
        functools.partial(_dispatch_kernel, n_tiles=n_tiles),
        grid_spec=pltpu.PrefetchScalarGridSpec(
            num_scalar_prefetch=3,
            grid=(nsteps,),
            in_specs=[
                pl.BlockSpec((tb, D_MODEL), lambda i, *_: (i, 0)),
                full((1, D_MODEL)),
                pl.BlockSpec((1, SUBLANES, tb), lambda i, *_: (i, 0, 0)),
                full((N_EXPERTS, LANES)),
            ],
            out_specs=[
                pl.BlockSpec(memory_space=pl.ANY),
                pl.BlockSpec((1, 8, tb), lambda i, *_: (i, 0, 0)),
            ],
            scratch_shapes=[
                pltpu.VMEM((2, tb * CHUNKS, LANES), F32),
                pltpu.VMEM((TE // 2, CHUNKS, LANES), F32),
                pltpu.VMEM((8, tb), I32),
                pltpu.SMEM((8, tb), I32),
                pltpu.SemaphoreType.DMA((2,)),
                pltpu.SemaphoreType.DMA,
            ],
        ),
        out_shape=[
            jax.ShapeDtypeStruct((n_tiles * TE, CHUNKS, LANES), F32),
            jax.ShapeDtypeStruct((nsteps, 8, tb), I32),
        ],
        compiler_params=pltpu.CompilerParams(
            dimension_semantics=("arbitrary",), vmem_limit_bytes=VMEM_LIMIT),
        name="dispatch",
    )(counts, offsets, nt, h, g, info, off_col)


def _expert_kernel(te_ref, nt_ref, xs_hbm, wg_ref, wu_ref, wd_ref, y_ref, wgu_s, wdb_s, xbuf_s, sems):
    i = pl.program_id(0)
    nt = nt_ref[0]
    rows = TE * CHUNKS

    def tile_copy(tile, slot):
        return pltpu.make_async_copy(xs_hbm.at[pl.ds(tile * rows, rows), :], xbuf_s.at[slot], sems.at[slot])

    @pl.when(i == 0)
    def _():
        for k in range(X_RING - 1):
            @pl.when(k < nt)
            def _(k=k):
                tile_copy(k, k).start()

    ahead = i + (X_RING - 1)

    @pl.when(ahead < nt)
    def _():
        tile_copy(ahead, ahead % X_RING).start()

    @pl.when(i >= nt)
    def _():
        y_ref[...] = jnp.zeros_like(y_ref)

    @pl.when(i < nt)
    def _():
        slot = i % X_RING
        tile_copy(i, slot).wait()
        x_ref = xbuf_s.at[slot]
        prev = te_ref[jnp.maximum(i - 1, 0)]

        @pl.when((i == 0) | (te_ref[i] != prev))
        def _():
            wgu_s[:, 0:D_EXPERT] = wg_ref[0].astype(BF16)
            wgu_s[:, D_EXPERT:2 * D_EXPERT] = wu_ref[0].astype(BF16)
            wdb_s[...] = wd_ref[0].astype(BF16)

        x = _load_token_tiles(x_ref, TE).astype(BF16)
        ab = jnp.dot(x, wgu_s[...], preferred_element_type=F32)
        a = ab[:, 0:D_EXPERT]
        b = ab[:, D_EXPERT:2 * D_EXPERT]
        hid = (a * jax.nn.sigmoid(a)) * b
        y = jnp.dot(hid.astype(BF16), wdb_s[...], preferred_element_type=F32)
        _store_token_tiles(y_ref, y)


def _experts(te, nt, xs, w_gate, w_up, w_down, n_tiles):
    tile = lambda i, te, nt: (i, 0)
    wblk = lambda i, te, nt: (te[i], 0, 0)
    return pl.pallas_call(
        _expert_kernel,
        grid_spec=pltpu.PrefetchScalarGridSpec(
            num_scalar_prefetch=2,
            grid=(n_tiles,),
            in_specs=[
                pl.BlockSpec(memory_space=pl.ANY),
                pl.BlockSpec((1, D_MODEL, D_EXPERT), wblk),
                pl.BlockSpec((1, D_MODEL, D_EXPERT), wblk),
                pl.BlockSpec((1, D_EXPERT, D_MODEL), wblk),
            ],
            out_specs=pl.BlockSpec((TE * CHUNKS, LANES), tile),
            scratch_shapes=[
                pltpu.VMEM((D_MODEL, 2 * D_EXPERT), BF16),
                pltpu.VMEM((D_EXPERT, D_MODEL), BF16),
                pltpu.VMEM((X_RING, TE * CHUNKS, LANES), F32),
                pltpu.SemaphoreType.DMA((X_RING,)),
            ],
        ),
        out_shape=jax.ShapeDtypeStruct(xs.shape, F32),
        compiler_params=pltpu.CompilerParams(
            dimension_semantics=("arbitrary",), vmem_limit_bytes=VMEM_LIMIT),
        name="experts",
    )(te, nt, xs, w_gate, w_up, w_down)


def _combine_kernel(dst0_ref, dstn_ref, h_ref, wts_ref, g_ref, ys_hbm, out_ref, dsts_s, ybuf_s, sem):
    i = pl.program_id(0)
    n_steps = pl.num_programs(0)
    tb = h_ref.shape[0]
    slot = i % 2

    def row_copy(buf, k, t, src_row):
        return pltpu.make_async_copy(ys_hbm.at[src_row], ybuf_s.at[buf, k, pl.ds(t * CHUNKS, CHUNKS), :], sem)

    def issue_tile(buf):
        for t in range(tb):
            row_copy(buf, 0, t, dsts_s[0, t]).start(priority=0)
            row_copy(buf, 1, t, dsts_s[1, t]).start(priority=1)

    def drain_tile():
        for _ in range(2 * tb):
            row_copy(0, 0, 0, 0).wait()

    @pl.when(i == 0)
    def _():
        pltpu.sync_copy(dst0_ref.at[0], dsts_s)
        issue_tile(0)

    pltpu.sync_copy(dstn_ref.at[0], dsts_s)
    drain_tile()

    for r0 in range(0, tb, COMB_ROWS):
        rows = slice(r0, r0 + COMB_ROWS)
        part = [_load_token_tiles(ybuf_s.at[slot, k, pl.ds(r0 * CHUNKS, COMB_ROWS * CHUNKS), :], COMB_ROWS)
                for k in range(2)]
        w = wts_ref[rows, :]
        moe = w[:, INFO_W1:INFO_W1 + 1] * part[0] + w[:, INFO_W2:INFO_W2 + 1] * part[1]
        out_ref[rows, :] = _rms(h_ref[rows, :] + moe, g_ref[...])
        for t in range(r0, r0 + COMB_ROWS):
            row_copy(1 - slot, 0, t, dsts_s[0, t]).start(priority=0)
            row_copy(1 - slot, 1, t, dsts_s[1, t]).start(priority=1)

    @pl.when(i == n_steps - 1)
    def _():
        drain_tile()


def _combine(dst, h, wts, g, ys):
    t = h.shape[0]
    tb = TB_COMB
    n_steps = t // tb
    return pl.pallas_call(
        _combine_kernel,
        grid=(n_steps,),
        in_specs=[
            pl.BlockSpec((1, 8, tb), lambda i: (0, 0, 0)),
            pl.BlockSpec((1, 8, tb), lambda i: (jnp.minimum(i + 1, n_steps - 1), 0, 0)),
            pl.BlockSpec((tb, D_MODEL), lambda i: (i, 0)),
            pl.BlockSpec((tb, LANES), lambda i: (i, 0)),
            pl.BlockSpec((1, D_MODEL), lambda i: (0, 0)),
            pl.BlockSpec(memory_space=pl.ANY),
        ],
        out_specs=pl.BlockSpec((tb, D_MODEL), lambda i: (i, 0)),
        out_shape=jax.ShapeDtypeStruct((t, D_MODEL), F32),
        scratch_shapes=[
            pltpu.SMEM((8, tb), I32),
            pltpu.VMEM((2, 2, tb * CHUNKS, LANES), F32),
            pltpu.SemaphoreType.DMA,
        ],
        compiler_params=pltpu.CompilerParams(
            dimension_semantics=("arbitrary",), vmem_limit_bytes=VMEM_LIMIT),
        name="combine",
    )(dst, dst, h, wts, g, ys)


def _tile_tables(counts, n_tiles):
    per = (counts + (TE - 1)) // TE
    cum = jnp.cumsum(per)
    total = cum[-1]
    offsets = ((cum - per) * TE).astype(I32)
    idx = jnp.minimum(jnp.arange(n_tiles, dtype=I32), total - 1)
    te = jnp.minimum(jnp.sum((idx[:, None] >= cum[None, :]).astype(I32), axis=1), N_EXPERTS - 1)
    return offsets, te, total.reshape(1).astype(I32)


def kernel(x, mix_norm_g, w_in, attn_sinks, w_spatial, b_spatial, gmlp_ln_g, gmlp_ln_b, attn_out_g, gmlp_out_g, w_out, ffn_norm_g, w_group_router, b_group_router, w_expert_router, b_expert_router, w_gate, w_up, w_down, final_norm_g):
    batch, seq, _ = x.shape
    t = batch * seq
    depth = mix_norm_g.shape[0]
    assert depth == 1, "the final norm is fused into the last layer's combine kernel"
    n_tiles = (2 * t) // TE + N_EXPERTS
    h = x.reshape(t, D_MODEL)
    for l in range(depth):
        n_pairs = GMLP_WIDTH // LANES
        wsp = w_spatial[l].reshape(n_pairs, 2, BLK, BLK).transpose(0, 2, 1, 3).reshape(n_pairs, BLK, 2 * BLK)
        bsp = jnp.repeat(b_spatial[l].T, HEAD_DIM, axis=1)
        h = _mix(h, attn_sinks[l], mix_norm_g[l][None], w_in[l].astype(BF16), wsp, bsp,
                 gmlp_ln_g[l][None], gmlp_ln_b[l][None], attn_out_g[l][None], gmlp_out_g[l][None],
                 w_out[l].astype(BF16), batch, seq)

        w_r = jnp.concatenate(
            [w_group_router[l], jnp.zeros((D_MODEL, EXPERT_ROW0 - N_GROUPS), F32),
             w_expert_router[l].transpose(1, 0, 2).reshape(D_MODEL, N_EXPERTS)], axis=1)
        w_r = jnp.pad(w_r, ((0, 0), (0, LANES - w_r.shape[1])))
        b_r = jnp.concatenate([b_group_router[l], jnp.zeros((EXPERT_ROW0 - N_GROUPS,), F32),
                               b_expert_router[l].reshape(-1)])
        b_r = jnp.pad(b_r, (0, LANES - b_r.shape[0]))[None]
        w_r_top = lax.bitcast_convert_type(
            lax.bitcast_convert_type(w_r, U32) & U32(0xFFFF0000), F32)
        w_r_hl = jnp.concatenate([w_r_top.astype(BF16), (w_r - w_r_top).astype(BF16)], axis=1)
        info, wts, cnt = _route(h, ffn_norm_g[l][None], w_r_hl, b_r)
        counts = cnt[:, 0].astype(I32)
        offsets, te, nt = _tile_tables(counts, n_tiles)
        off_col = jnp.broadcast_to(offsets.astype(F32)[:, None], (N_EXPERTS, LANES))
        xs, dst = _dispatch(counts, offsets, nt, h, ffn_norm_g[l][None], info, off_col, n_tiles)
        ys = _experts(te, nt, xs.reshape(-1, LANES), w_gate[l], w_up[l], w_down[l], n_tiles)
        h = _combine(dst, h, wts, final_norm_g[None], ys.reshape(xs.shape))
    return h.reshape(batch, seq, D_MODEL)
```

```python
import functools

import jax
import jax.numpy as jnp
from jax import lax
from jax.experimental import pallas as pl
from jax.experimental.pallas import tpu as pltpu

F32 = jnp.float32
BF16 = jnp.bfloat16
I32 = jnp.int32
U32 = jnp.uint32

D_MODEL = 1024
HEAD_DIM = 64
ATTN_WIDTH = 512
N_Q_HEADS = 8
N_KV_HEADS = 2
Q_PER_KV = 4
KV_WIDTH = 128
BLK = 128
GMLP_WIDTH = 512
IN_PROJ = 1792
N_GROUPS = 4
E_PER_GROUP = 8
N_EXPERTS = 32
D_EXPERT = 256
RMS_EPS = 1e-6
LN_EPS = 1e-5
NEG_INF = -1e30

LANES = 128
SUBLANES = 8
CHUNKS = D_MODEL // LANES
TM_MIX = 512
TB_ROUTE = 512
TE = 256
X_RING = 3
TB_COMB = 512
COMB_ROWS = 32
DISP_ROWS = 32
VMEM_LIMIT = 56 * 1024 * 1024

Q_OFF, K_OFF, V_OFF = 0, 512, 640
GU_OFF, GV_OFF = 768, 1280
GROUP_ROW0, EXPERT_ROW0 = 0, 8


def _rms(x, g):
    return (x * lax.rsqrt(jnp.mean(x * x, axis=-1, keepdims=True) + RMS_EPS)) * g


def _gelu(x):
    return 0.5 * x * (1.0 + lax.erf(x * F32(0.7071067811865476)))


def _store_token_tiles(ref2d, val):
    n = val.shape[0]
    for i in range(n // SUBLANES):
        for c in range(CHUNKS):
            ref2d[pl.ds(i * SUBLANES * CHUNKS + c, SUBLANES, stride=CHUNKS), :] = (
                val[i * SUBLANES:(i + 1) * SUBLANES, c * LANES:(c + 1) * LANES])


def _load_token_tiles(ref2d, n):
    cols = []
    for c in range(CHUNKS):
        cols.append(jnp.concatenate(
            [ref2d[pl.ds(i * SUBLANES * CHUNKS + c, SUBLANES, stride=CHUNKS), :] for i in range(n // SUBLANES)],
            axis=0))
    return jnp.concatenate(cols, axis=1)


def _mix_kernel(sinks_ref, x_ref, g1_ref, win_ref, wsp_ref, bsp_ref, lng_ref, lnb_ref,
                ag_ref, gg_ref, wout_ref, h_ref, proj_s, mixed_s, k4p_s, v4p_s):
    j = pl.program_id(1)
    tm = x_ref.shape[0]

    @pl.when(j == 0)
    def _():
        k4p_s[...] = jnp.zeros_like(k4p_s)
        v4p_s[...] = jnp.zeros_like(v4p_s)

    xn = _rms(x_ref[...], g1_ref[...])
    proj_s[...] = jnp.dot(xn.astype(BF16), win_ref[...], preferred_element_type=F32)

    lane128 = lax.broadcasted_iota(I32, (BLK, LANES), 1)
    lo_half = lane128 < HEAD_DIM
    lane256 = lax.broadcasted_iota(I32, (BLK, 2 * LANES), 1)
    head_masks = [(lane256 >= h * HEAD_DIM) & (lane256 < (h + 1) * HEAD_DIM) for h in range(Q_PER_KV)]
    key = lax.broadcasted_iota(I32, (2 * BLK, Q_PER_KV * BLK), 0)
    qry = lax.broadcasted_iota(I32, (2 * BLK, Q_PER_KV * BLK), 1) & (BLK - 1)
    in_window = (key > qry) & (key <= qry + BLK)
    first_ok = in_window & ((key >= BLK) | (j > 0))

    trow = lax.broadcasted_iota(I32, (BLK, 2 * BLK), 0)
    tcol = lax.broadcasted_iota(I32, (BLK, 2 * BLK), 1) & (BLK - 1)
    causal2 = tcol <= trow
    wsp = [jnp.where(causal2, wsp_ref[p], 0.0).astype(BF16) for p in range(GMLP_WIDTH // LANES)]

    scale = F32(HEAD_DIM ** -0.5)
    k4_prev = [k4p_s[kvh] for kvh in range(N_KV_HEADS)]
    v4t_prev = [v4p_s[kvh] for kvh in range(N_KV_HEADS)]
    for i in range(tm // BLK):
        r0 = i * BLK
        kk = proj_s[r0:r0 + BLK, K_OFF:K_OFF + KV_WIDTH]
        vv = proj_s[r0:r0 + BLK, V_OFF:V_OFF + KV_WIDTH]
        kk_r = pltpu.roll(kk, HEAD_DIM, 1)
        vv_r = pltpu.roll(vv, HEAD_DIM, 1)
        o_parts = []
        for kvh in range(N_KV_HEADS):
            k2 = jnp.where(lo_half, kk, kk_r) if kvh == 0 else jnp.where(lo_half, kk_r, kk)
            v2 = jnp.where(lo_half, vv, vv_r) if kvh == 0 else jnp.where(lo_half, vv_r, vv)
            k4 = jnp.concatenate([k2, k2], axis=1).astype(BF16)
            v4t = jnp.transpose(jnp.concatenate([v2, v2], axis=1)).astype(BF16)
            kband = jnp.concatenate([k4_prev[kvh], k4], axis=0)
            vband_t = jnp.concatenate([v4t_prev[kvh], v4t], axis=1)
            k4_prev[kvh] = k4
            v4t_prev[kvh] = v4t
            q4 = proj_s[r0:r0 + BLK, Q_OFF + kvh * 256:Q_OFF + (kvh + 1) * 256] * scale
            qs = jnp.concatenate([jnp.where(m, q4, 0.0) for m in head_masks], axis=0).astype(BF16)
            st = lax.dot_general(kband, qs, (((1,), (1,)), ((), ())),
                                 preferred_element_type=F32)
            st = jnp.where(first_ok if i == 0 else in_window, st, NEG_INF)
            sink = jnp.concatenate(
                [jnp.full((1, BLK), sinks_ref[kvh * Q_PER_KV + h], F32) for h in range(Q_PER_KV)], axis=1)
            m = jnp.maximum(jnp.max(st, axis=0, keepdims=True), sink)
            p = jnp.exp(st - m)
            denom = jnp.sum(p, axis=0, keepdims=True) + jnp.exp(sink - m)
            pvt = jnp.dot(vband_t, p.astype(BF16), preferred_element_type=F32)
            inv_denom = 1.0 / denom
            zt = jnp.concatenate([pvt[h * HEAD_DIM:(h + 1) * HEAD_DIM, h * BLK:(h + 1) * BLK]
                                  * inv_denom[:, h * BLK:(h + 1) * BLK] for h in range(Q_PER_KV)], axis=0)
            o_parts.append(jnp.transpose(zt))
        a = jnp.concatenate(o_parts, axis=1)
        mixed_s[r0:r0 + BLK, 0:ATTN_WIDTH] = _rms(a, ag_ref[...]).astype(BF16)

        u = _gelu(proj_s[r0:r0 + BLK, GU_OFF:GU_OFF + GMLP_WIDTH])
        vg = _gelu(proj_s[r0:r0 + BLK, GV_OFF:GV_OFF + GMLP_WIDTH])
        mu = jnp.mean(vg, axis=-1, keepdims=True)
        var = jnp.mean(jnp.square(vg - mu), axis=-1, keepdims=True)
        vln = (vg - mu) * lax.rsqrt(var + LN_EPS) * lng_ref[...] + lnb_ref[...]
        s_parts = []
        for pidx in range(GMLP_WIDTH // LANES):
            vcol = vln[:, pidx * LANES:(pidx + 1) * LANES]
            rhs = jnp.concatenate([jnp.where(lo_half, vcol, 0.0), jnp.where(lo_half, 0.0, vcol)],
                                  axis=0).astype(BF16)
            s_parts.append(jnp.dot(wsp[pidx], rhs, preferred_element_type=F32))
        sg = jnp.concatenate(s_parts, axis=1) + bsp_ref[...]
        g = u * sg
        mixed_s[r0:r0 + BLK, ATTN_WIDTH:D_MODEL] = _rms(g, gg_ref[...]).astype(BF16)

    for kvh in range(N_KV_HEADS):
        k4p_s[kvh] = k4_prev[kvh]
        v4p_s[kvh] = v4t_prev[kvh]
    h_ref[...] = x_ref[...] + jnp.dot(mixed_s[...], wout_ref[...], preferred_element_type=F32)


def _mix(x2, sinks, g1, w_in, wsp, bsp, lng, lnb, ag, gg, w_out, batch, seq):
    t = x2.shape[0]
    tm = TM_MIX
    nj = seq // tm
    full = lambda shape: pl.BlockSpec(shape, lambda b, j: (0,) * len(shape))
    return pl.pallas_call(
        _mix_kernel,
        grid=(batch, nj),
        in_specs=[
            pl.BlockSpec(memory_space=pltpu.SMEM),
            pl.BlockSpec((tm, D_MODEL), lambda b, j: (b * nj + j, 0)),
            full((1, D_MODEL)),
            full((D_MODEL, IN_PROJ)),
            full((GMLP_WIDTH // LANES, BLK, 2 * BLK)),
            full((BLK, GMLP_WIDTH)),
            full((1, GMLP_WIDTH)),
            full((1, GMLP_WIDTH)),
            full((1, ATTN_WIDTH)),
            full((1, GMLP_WIDTH)),
            full((D_MODEL, D_MODEL)),
        ],
        out_specs=pl.BlockSpec((tm, D_MODEL), lambda b, j: (b * nj + j, 0)),
        out_shape=jax.ShapeDtypeStruct((t, D_MODEL), F32),
        scratch_shapes=[
            pltpu.VMEM((tm, IN_PROJ), F32),
            pltpu.VMEM((tm, D_MODEL), BF16),
            pltpu.VMEM((N_KV_HEADS, BLK, 2 * LANES), BF16),
            pltpu.VMEM((N_KV_HEADS, 2 * LANES, BLK), BF16),
        ],
        compiler_params=pltpu.CompilerParams(
            dimension_semantics=("arbitrary", "arbitrary"), vmem_limit_bytes=VMEM_LIMIT),
        name="mix",
    )(sinks, x2, g1, w_in, wsp, bsp, lng, lnb, ag, gg, w_out)


INFO_W1, INFO_W2, INFO_E1, INFO_E2, INFO_R1, INFO_R2 = range(6)


def _route_kernel(h_ref, g_ref, whl_ref, b_ref, info_ref, wts_ref, cnt_ref, carry_s):
    i = pl.program_id(0)
    tb = h_ref.shape[0]

    @pl.when(i == 0)
    def _():
        carry_s[...] = jnp.zeros_like(carry_s)

    hn = _rms(h_ref[...], g_ref[...])
    hi = hn.astype(BF16)
    lo = (hn - hi.astype(F32)).astype(BF16)
    hi_both = jnp.dot(hi, whl_ref[...], preferred_element_type=F32)
    logits = (hi_both[:, 0:LANES]
              + (hi_both[:, LANES:2 * LANES] + jnp.dot(lo, whl_ref[:, 0:LANES], preferred_element_type=F32))
              + b_ref[...])
    lt = jnp.transpose(logits)
    row8 = lax.broadcasted_iota(I32, (SUBLANES, tb), 0)
    ninf = F32(-jnp.inf)

    def top_row(vals):
        best = jnp.max(vals, axis=0, keepdims=True)
        return best, jnp.min(jnp.where(vals == best, row8, SUBLANES), axis=0, keepdims=True)

    gl = jnp.where(row8 < N_GROUPS, lt[GROUP_ROW0:GROUP_ROW0 + SUBLANES], ninf)
    gmax, gidx = top_row(gl)
    g_w = 1.0 / jnp.sum(jnp.exp(gl - gmax), axis=0, keepdims=True)

    el = lt[EXPERT_ROW0:EXPERT_ROW0 + E_PER_GROUP]
    for g in range(1, N_GROUPS):
        el = jnp.where(gidx == g, lt[EXPERT_ROW0 + g * E_PER_GROUP:EXPERT_ROW0 + (g + 1) * E_PER_GROUP], el)
    v1, i1 = top_row(el)
    v2, i2 = top_row(jnp.where(row8 == i1, ninf, el))
    t2 = jnp.exp(v2 - v1)
    inv = 1.0 / (1.0 + t2)
    w1 = inv * g_w
    w2 = (t2 * inv) * g_w
    e1 = gidx * E_PER_GROUP + i1
    e2 = gidx * E_PER_GROUP + i2

    rowe = lax.broadcasted_iota(I32, (N_EXPERTS, tb), 0)
    sel1 = rowe == e1
    sel2 = rowe == e2
    oh = jnp.where(sel1 | sel2, 1.0, 0.0)
    rr = lax.broadcasted_iota(I32, (tb, tb), 0)
    cc = lax.broadcasted_iota(I32, (tb, tb), 1)
    triu = jnp.where(rr <= cc, 1.0, 0.0).astype(BF16)
    incl = jnp.dot(oh.astype(BF16), triu, preferred_element_type=F32)
    carry = carry_s[...]
    before = incl - oh + carry[:, 0:1]
    rank1 = jnp.sum(jnp.where(sel1, before, 0.0), axis=0, keepdims=True)
    rank2 = jnp.sum(jnp.where(sel2, before, 0.0), axis=0, keepdims=True)
    carry = carry + incl[:, tb - 1:tb]
    carry_s[...] = carry
    cnt_ref[...] = carry

    rows = (w1, w2, e1.astype(F32), e2.astype(F32), rank1, rank2)
    info = jnp.zeros((SUBLANES, tb), F32)
    for k, val in enumerate(rows):
        info = jnp.where(row8 == k, val, info)
    info_ref[0] = info
    row128 = lax.broadcasted_iota(I32, (LANES, tb), 0)
    wts_ref[...] = jnp.transpose(jnp.where(row128 == INFO_W1, w1, jnp.where(row128 == INFO_W2, w2, 0.0)))


def _route(h, g, whl, bias):
    t = h.shape[0]
    tb = TB_ROUTE
    nsteps = t // tb
    full = lambda shape: pl.BlockSpec(shape, lambda i: (0,) * len(shape))
    return pl.pallas_call(
        _route_kernel,
        grid=(nsteps,),
        in_specs=[
            pl.BlockSpec((tb, D_MODEL), lambda i: (i, 0)),
            full((1, D_MODEL)),
            full((D_MODEL, 2 * LANES)),
            full((1, LANES)),
        ],
        out_specs=[
            pl.BlockSpec((1, SUBLANES, tb), lambda i: (i, 0, 0)),
            pl.BlockSpec((tb, LANES), lambda i: (i, 0)),
            full((N_EXPERTS, LANES)),
        ],
        out_shape=[
            jax.ShapeDtypeStruct((nsteps, SUBLANES, tb), F32),
            jax.ShapeDtypeStruct((t, LANES), F32),
            jax.ShapeDtypeStruct((N_EXPERTS, LANES), F32),
        ],
        scratch_shapes=[pltpu.VMEM((N_EXPERTS, LANES), F32)],
        compiler_params=pltpu.CompilerParams(
            dimension_semantics=("arbitrary",), vmem_limit_bytes=VMEM_LIMIT),
        name="route",
    )(h, g, whl, bias)


def _dispatch_kernel(cnt_ref, off_ref, nt_ref, h_ref, g_ref, info_ref, offcol_ref, xs_hbm, dst_ref,
                     rows_s, zeros_s, dstv_s, dsts_s, sems, fill_sem, *, n_tiles):
    i = pl.program_id(0)
    n_steps = pl.num_programs(0)
    tb = h_ref.shape[0]
    slot = i % 2

    def row_copy(buf, t, dst_row):
        return pltpu.make_async_copy(rows_s.at[buf, pl.ds(t * CHUNKS, CHUNKS), :], xs_hbm.at[dst_row],
                                     sems.at[buf])

    def drain_tile(buf):
        for _ in range(2 * tb):
            row_copy(buf, 0, 0).wait()

    @pl.when(i >= 2)
    def _():
        drain_tile(slot)

    info = info_ref[0]
    rowe = lax.broadcasted_iota(I32, (N_EXPERTS, tb), 0).astype(F32)
    off = offcol_ref[...][:, 0:1]
    dst_rows = []
    for ke, kr in ((INFO_E1, INFO_R1), (INFO_E2, INFO_R2)):
        base = jnp.sum(jnp.where(rowe == info[ke:ke + 1, :], off, 0.0), axis=0, keepdims=True)
        dst_rows.append(base + info[kr:kr + 1, :])
    row8 = lax.broadcasted_iota(I32, (SUBLANES, tb), 0)
    dst_t = jnp.where(row8 == 0, dst_rows[0], jnp.where(row8 == 1, dst_rows[1], 0.0)).astype(I32)
    dstv_s[...] = dst_t
    dst_ref[0] = dst_t
    pltpu.sync_copy(dstv_s, dsts_s)

    for r0 in range(0, tb, DISP_ROWS):
        _store_token_tiles(rows_s.at[slot, pl.ds(r0 * CHUNKS, DISP_ROWS * CHUNKS), :],
                           _rms(h_ref[r0:r0 + DISP_ROWS, :], g_ref[...]))
        for t in range(r0, r0 + DISP_ROWS):
            row_copy(slot, t, dsts_s[0, t]).start(priority=0)
            row_copy(slot, t, dsts_s[1, t]).start(priority=1)

    @pl.when(i == n_steps - 1)
    def _():
        zeros_s[...] = jnp.zeros_like(zeros_s)
        half = TE // 2

        def fill_copy(pos, size):
            return pltpu.make_async_copy(zeros_s.at[pl.ds(0, size)], xs_hbm.at[pl.ds(pos, size)], fill_sem)

        def for_each_gap(action):
            for e in range(N_EXPERTS):
                n = cnt_ref[e]
                gap = (-n) & (TE - 1)
                pos = off_ref[e] + n
                size = half
                while size >= 1:
                    @pl.when((gap & size) != 0)
                    def _(pos=pos, size=size):
                        action(fill_copy(pos, size))
                    pos = pos + (gap & size)
                    size //= 2

        def tail_issue(k, c):
            fill_copy(k * half, half).start()
            return c

        def tail_wait(k, c):
            fill_copy(0, half).wait()
            return c

        for_each_gap(lambda cp: cp.start())
        lax.fori_loop(2 * nt_ref[0], 2 * n_tiles, tail_issue, 0)
        for_each_gap(lambda cp: cp.wait())
        lax.fori_loop(2 * nt_ref[0], 2 * n_tiles, tail_wait, 0)
        drain_tile(slot)

        @pl.when(i >= 1)
        def _():
            drain_tile(1 - slot)


def _dispatch(counts, offsets, nt, h, g, info, off_col, n_tiles):
    t = h.shape[0]
    tb = TB_ROUTE
    nsteps = t // tb
    full = lambda shape: pl.BlockSpec(shape, lambda i, *_: (0,) * len(shape))
    return pl.pallas_call(
        functools.partial(_dispatch_kernel, n_tiles=n_tiles),
        grid_spec=pltpu.PrefetchScalarGridSpec(
            num_scalar_prefetch=3,
            grid=(nsteps,),
            in_specs=[
                pl.BlockSpec((tb, D_MODEL), lambda i, *_: (i, 0)),
                full((1, D_MODEL)),
                pl.BlockSpec((1, SUBLANES, tb), lambda i, *_: (i, 0, 0)),
                full((N_EXPERTS, LANES)),
            ],
            out_specs=[
                pl.BlockSpec(memory_space=pl.ANY),
                pl.BlockSpec((1, 8, tb), lambda i, *_: (i, 0, 0)),
            ],
            scratch_shapes=[
                pltpu.VMEM((2, tb * CHUNKS, LANES), F32),
                pltpu.VMEM((TE // 2, CHUNKS, LANES), F32),
                pltpu.VMEM((8, tb), I32),
                pltpu.SMEM((8, tb), I32),
                pltpu.SemaphoreType.DMA((2,)),
                pltpu.SemaphoreType.DMA,
            ],
        ),
        out_shape=[
            jax.ShapeDtypeStruct((n_tiles * TE, CHUNKS, LANES), F32),
            jax.ShapeDtypeStruct((nsteps, 8, tb), I32),
        ],
        compiler_params=pltpu.CompilerParams(
            dimension_semantics=("arbitrary",), vmem_limit_bytes=VMEM_LIMIT),
        name="dispatch",
    )(counts, offsets, nt, h, g, info, off_col)


def _expert_kernel(te_ref, nt_ref, xs_hbm, wg_ref, wu_ref, wd_ref, y_ref, wgu_s, wdb_s, xbuf_s, sems):
    i = pl.program_id(0)
    nt = nt_ref[0]
    rows = TE * CHUNKS

    def tile_copy(tile, slot):
        return pltpu.make_async_copy(xs_hbm.at[pl.ds(tile * rows, rows), :], xbuf_s.at[slot], sems.at[slot])

    @pl.when(i == 0)
    def _():
        for k in range(X_RING - 1):
            @pl.when(k < nt)
            def _(k=k):
                tile_copy(k, k).start()

    ahead = i + (X_RING - 1)

    @pl.when(ahead < nt)
    def _():
        tile_copy(ahead, ahead % X_RING).start()

    @pl.when(i >= nt)
    def _():
        y_ref[...] = jnp.zeros_like(y_ref)

    @pl.when(i < nt)
    def _():
        slot = i % X_RING
        tile_copy(i, slot).wait()
        x_ref = xbuf_s.at[slot]
        prev = te_ref[jnp.maximum(i - 1, 0)]

        @pl.when((i == 0) | (te_ref[i] != prev))
        def _():
            wgu_s[:, 0:D_EXPERT] = wg_ref[0].astype(BF16)
            wgu_s[:, D_EXPERT:2 * D_EXPERT] = wu_ref[0].astype(BF16)
            wdb_s[...] = wd_ref[0].astype(BF16)

        x = _load_token_tiles(x_ref, TE).astype(BF16)
        ab = jnp.dot(x, wgu_s[...], preferred_element_type=F32)
        a = ab[:, 0:D_EXPERT]
        b = ab[:, D_EXPERT:2 * D_EXPERT]
        hid = (a * jax.nn.sigmoid(a)) * b
        y = jnp.dot(hid.astype(BF16), wdb_s[...], preferred_element_type=F32)
        _store_token_tiles(y_ref, y)


def _experts(te, nt, xs, w_gate, w_up, w_down, n_tiles):
    tile = lambda i, te, nt: (i, 0)
    wblk = lambda i, te, nt: (te[i], 0, 0)
    return pl.pallas_call(
        _expert_kernel,
        grid_spec=pltpu.PrefetchScalarGridSpec(
            num_scalar_prefetch=2,
            grid=(n_tiles,),
            in_specs=[
                pl.BlockSpec(memory_space=pl.ANY),
                pl.BlockSpec((1, D_MODEL, D_EXPERT), wblk),
                pl.BlockSpec((1, D_MODEL, D_EXPERT), wblk),
                pl.BlockSpec((1, D_EXPERT, D_MODEL), wblk),
            ],
            out_specs=pl.BlockSpec((TE * CHUNKS, LANES), tile),
            scratch_shapes=[
                pltpu.VMEM((D_MODEL, 2 * D_EXPERT), BF16),
                pltpu.VMEM((D_EXPERT, D_MODEL), BF16),
                pltpu.VMEM((X_RING, TE * CHUNKS, LANES), F32),
                pltpu.SemaphoreType.DMA((X_RING,)),
            ],
        ),
        out_shape=jax.ShapeDtypeStruct(xs.shape, F32),
        compiler_params=pltpu.CompilerParams(
            dimension_semantics=("arbitrary",), vmem_limit_bytes=VMEM_LIMIT),
        name="experts",
    )(te, nt, xs, w_gate, w_up, w_down)


def _combine_kernel(dst0_ref, dstn_ref, h_ref, wts_ref, g_ref, ys_hbm, out_ref, dsts_s, ybuf_s, sem):
    i = pl.program_id(0)
    n_steps = pl.num_programs(0)
    tb = h_ref.shape[0]
    slot = i % 2

    def row_copy(buf, k, t, src_row):
        return pltpu.make_async_copy(ys_hbm.at[src_row], ybuf_s.at[buf, k, pl.ds(t * CHUNKS, CHUNKS), :], sem)

    def issue_tile(buf):
        for t in range(tb):
            row_copy(buf, 0, t, dsts_s[0, t]).start(priority=0)
            row_copy(buf, 1, t, dsts_s[1, t]).start(priority=1)

    def drain_tile():
        for _ in range(2 * tb):
            row_copy(0, 0, 0, 0).wait()

    @pl.when(i == 0)
    def _():
        pltpu.sync_copy(dst0_ref.at[0], dsts_s)
        issue_tile(0)

    pltpu.sync_copy(dstn_ref.at[0], dsts_s)
    drain_tile()

    for r0 in range(0, tb, COMB_ROWS):
        rows = slice(r0, r0 + COMB_ROWS)
        part = [_load_token_tiles(ybuf_s.at[slot, k, pl.ds(r0 * CHUNKS, COMB_ROWS * CHUNKS), :], COMB_ROWS)
                for k in range(2)]
        w = wts_ref[rows, :]
        moe = w[:, INFO_W1:INFO_W1 + 1] * part[0] + w[:, INFO_W2:INFO_W2 + 1] * part[1]
        out_ref[rows, :] = _rms(h_ref[rows, :] + moe, g_ref[...])
        for t in range(r0, r0 + COMB_ROWS):
            row_copy(1 - slot, 0, t, dsts_s[0, t]).start(priority=0)
            row_copy(1 - slot, 1, t, dsts_s[1, t]).start(priority=1)

    @pl.when(i == n_steps - 1)
    def _():
        drain_tile()


def _combine(dst, h, wts, g, ys):
    t = h.shape[0]
    tb = TB_COMB
    n_steps = t // tb
    return pl.pallas_call(
        _combine_kernel,
        grid=(n_steps,),
        in_specs=[
            pl.BlockSpec((1, 8, tb), lambda i: (0, 0, 0)),
            pl.BlockSpec((1, 8, tb), lambda i: (jnp.minimum(i + 1, n_steps - 1), 0, 0)),
            pl.BlockSpec((tb, D_MODEL), lambda i: (i, 0)),
            pl.BlockSpec((tb, LANES), lambda i: (i, 0)),
            pl.BlockSpec((1, D_MODEL), lambda i: (0, 0)),
            pl.BlockSpec(memory_space=pl.ANY),
        ],
        out_specs=pl.BlockSpec((tb, D_MODEL), lambda i: (i, 0)),
        out_shape=jax.ShapeDtypeStruct((t, D_MODEL), F32),
        scratch_shapes=[
            pltpu.SMEM((8, tb), I32),
            pltpu.VMEM((2, 2, tb * CHUNKS, LANES), F32),
            pltpu.SemaphoreType.DMA,
        ],
        compiler_params=pltpu.CompilerParams(
            dimension_semantics=("arbitrary",), vmem_limit_bytes=VMEM_LIMIT),
        name="combine",
    )(dst, dst, h, wts, g, ys)


def _tile_tables(counts, n_tiles):
    per = (counts + (TE - 1)) // TE
    cum = jnp.cumsum(per)
    total = cum[-1]
    offsets = ((cum - per) * TE).astype(I32)
    idx = jnp.minimum(jnp.arange(n_tiles, dtype=I32), total - 1)
    te = jnp.minimum(jnp.sum((idx[:, None] >= cum[None, :]).astype(I32), axis=1), N_EXPERTS - 1)
    return offsets, te, total.reshape(1).astype(I32)


def kernel(x, mix_norm_g, w_in, attn_sinks, w_spatial, b_spatial, gmlp_ln_g, gmlp_ln_b, attn_out_g, gmlp_out_g, w_out, ffn_norm_g, w_group_router, b_group_router, w_expert_router, b_expert_router, w_gate, w_up, w_down, final_norm_g):
    batch, seq, _ = x.shape
    t = batch * seq
    depth = mix_norm_g.shape[0]
    assert depth == 1, "the final norm is fused into the last layer's combine kernel"
    n_tiles = (2 * t) // TE + N_EXPERTS
    h = x.reshape(t, D_MODEL)
    for l in range(depth):
        n_pairs = GMLP_WIDTH // LANES
        wsp = w_spatial[l].reshape(n_pairs, 2, BLK, BLK).transpose(0, 2, 1, 3).reshape(n_pairs, BLK, 2 * BLK)
        bsp = jnp.repeat(b_spatial[l].T, HEAD_DIM, axis=1)
        h = _mix(h, attn_sinks[l], mix_norm_g[l][None], w_in[l].astype(BF16), wsp, bsp,
                 gmlp_ln_g[l][None], gmlp_ln_b[l][None], attn_out_g[l][None], gmlp_out_g[l][None],
                 w_out[l].astype(BF16), batch, seq)

        w_r = jnp.concatenate(
            [w_group_router[l], jnp.zeros((D_MODEL, EXPERT_ROW0 - N_GROUPS), F32),
             w_expert_router[l].transpose(1, 0, 2).reshape(D_MODEL, N_EXPERTS)], axis=1)
        w_r = jnp.pad(w_r, ((0, 0), (0, LANES - w_r.shape[1])))
        b_r = jnp.concatenate([b_group_router[l], jnp.zeros((EXPERT_ROW0 - N_GROUPS,), F32),
                               b_expert_router[l].reshape(-1)])
        b_r = jnp.pad(b_r, (0, LANES - b_r.shape[0]))[None]
        w_r_top = lax.bitcast_convert_type(
            lax.bitcast_convert_type(w_r, U32) & U32(0xFFFF0000), F32)
        w_r_hl = jnp.concatenate([w_r_top.astype(BF16), (w_r - w_r_top).astype(BF16)], axis=1)
        info, wts, cnt = _route(h, ffn_norm_g[l][None], w_r_hl, b_r)
        counts = cnt[:, 0].astype(I32)
        offsets, te, nt = _tile_tables(counts, n_tiles)
        off_col = jnp.broadcast_to(offsets.astype(F32)[:, None], (N_EXPERTS, LANES))
        xs, dst = _dispatch(counts, offsets, nt, h, ffn_norm_g[l][None], info, off_col, n_tiles)
        ys = _experts(te, nt, xs.reshape(-1, LANES), w_gate[l], w_up[l], w_down[l], n_tiles)
        h = _combine(dst, h, wts, final_norm_g[None], ys.reshape(xs.shape))
    return h.reshape(batch, seq, D_MODEL)
```

```python
import functools

import jax
import jax.numpy as jnp
from jax import lax
from jax.experimental import pallas as pl
from jax.experimental.pallas import tpu as pltpu

F32 = jnp.float32
BF16 = jnp.bfloat16
I32 = jnp.int32
U32 = jnp.uint32

D_MODEL = 1024
HEAD_DIM = 64
ATTN_WIDTH = 512
N_Q_HEADS = 8
N_KV_HEADS = 2
Q_PER_KV = 4
KV_WIDTH = 128
BLK = 128
GMLP_WIDTH = 512
IN_PROJ = 1792
N_GROUPS = 4
E_PER_GROUP = 8
N_EXPERTS = 32
D_EXPERT = 256
RMS_EPS = 1e-6
LN_EPS = 1e-5
NEG_INF = -1e30

LANES = 128
SUBLANES = 8
CHUNKS = D_MODEL // LANES
TM_MIX = 512
TB_ROUTE = 512
TE = 512
X_RING = 3
TB_COMB = 512
COMB_ROWS = 32
DISP_ROWS = 32
VMEM_LIMIT = 56 * 1024 * 1024

Q_OFF, K_OFF, V_OFF = 0, 512, 640
GU_OFF, GV_OFF = 768, 1280
GROUP_ROW0, EXPERT_ROW0 = 0, 8


def _rms(x, g):
    return (x * lax.rsqrt(jnp.mean(x * x, axis=-1, keepdims=True) + RMS_EPS)) * g


def _gelu(x):
    return 0.5 * x * (1.0 + lax.erf(x * F32(0.7071067811865476)))


def _store_token_tiles(ref2d, val):
    n = val.shape[0]
    for i in range(n // SUBLANES):
        for c in range(CHUNKS):
            ref2d[pl.ds(i * SUBLANES * CHUNKS + c, SUBLANES, stride=CHUNKS), :] = (
                val[i * SUBLANES:(i + 1) * SUBLANES, c * LANES:(c + 1) * LANES])


def _load_token_tiles(ref2d, n):
    cols = []
    for c in range(CHUNKS):
        cols.append(jnp.concatenate(
            [ref2d[pl.ds(i * SUBLANES * CHUNKS + c, SUBLANES, stride=CHUNKS), :] for i in range(n // SUBLANES)],
            axis=0))
    return jnp.concatenate(cols, axis=1)


def _mix_kernel(sinks_ref, x_ref, g1_ref, win_ref, wsp_ref, bsp_ref, lng_ref, lnb_ref,
                ag_ref, gg_ref, wout_ref, rg_ref, whl_ref, rb_ref, h_ref, info_ref, wts_ref, cnt_ref,
                proj_s, mixed_s, k4p_s, v4p_s, carry_s):
    j = pl.program_id(1)
    tm = x_ref.shape[0]

    @pl.when(j == 0)
    def _():
        k4p_s[...] = jnp.zeros_like(k4p_s)
        v4p_s[...] = jnp.zeros_like(v4p_s)

    xn = _rms(x_ref[...], g1_ref[...])
    proj_s[...] = jnp.dot(xn.astype(BF16), win_ref[...], preferred_element_type=F32)

    lane128 = lax.broadcasted_iota(I32, (BLK, LANES), 1)
    lo_half = lane128 < HEAD_DIM
    lane256 = lax.broadcasted_iota(I32, (BLK, 2 * LANES), 1)
    head_masks = [(lane256 >= h * HEAD_DIM) & (lane256 < (h + 1) * HEAD_DIM) for h in range(Q_PER_KV)]
    key = lax.broadcasted_iota(I32, (2 * BLK, Q_PER_KV * BLK), 0)
    qry = lax.broadcasted_iota(I32, (2 * BLK, Q_PER_KV * BLK), 1) & (BLK - 1)
    in_window = (key > qry) & (key <= qry + BLK)
    first_ok = in_window & ((key >= BLK) | (j > 0))

    trow = lax.broadcasted_iota(I32, (BLK, 2 * BLK), 0)
    tcol = lax.broadcasted_iota(I32, (BLK, 2 * BLK), 1) & (BLK - 1)
    causal2 = tcol <= trow
    wsp = [jnp.where(causal2, wsp_ref[p], 0.0).astype(BF16) for p in range(GMLP_WIDTH // LANES)]

    scale = F32(HEAD_DIM ** -0.5)
    k4_prev = [k4p_s[kvh] for kvh in range(N_KV_HEADS)]
    v4t_prev = [v4p_s[kvh] for kvh in range(N_KV_HEADS)]
    for i in range(tm // BLK):
        r0 = i * BLK
        kk = proj_s[r0:r0 + BLK, K_OFF:K_OFF + KV_WIDTH]
        vv = proj_s[r0:r0 + BLK, V_OFF:V_OFF + KV_WIDTH]
        kk_r = pltpu.roll(kk, HEAD_DIM, 1)
        vv_r = pltpu.roll(vv, HEAD_DIM, 1)
        o_parts = []
        for kvh in range(N_KV_HEADS):
            k2 = jnp.where(lo_half, kk, kk_r) if kvh == 0 else jnp.where(lo_half, kk_r, kk)
            v2 = jnp.where(lo_half, vv, vv_r) if kvh == 0 else jnp.where(lo_half, vv_r, vv)
            k4 = jnp.concatenate([k2, k2], axis=1).astype(BF16)
            v4t = jnp.transpose(jnp.concatenate([v2, v2], axis=1)).astype(BF16)
            kband = jnp.concatenate([k4_prev[kvh], k4], axis=0)
            vband_t = jnp.concatenate([v4t_prev[kvh], v4t], axis=1)
            k4_prev[kvh] = k4
            v4t_prev[kvh] = v4t
            q4 = proj_s[r0:r0 + BLK, Q_OFF + kvh * 256:Q_OFF + (kvh + 1) * 256] * scale
            qs = jnp.concatenate([jnp.where(m, q4, 0.0) for m in head_masks], axis=0).astype(BF16)
            st = lax.dot_general(kband, qs, (((1,), (1,)), ((), ())),
                                 preferred_element_type=F32)
            st = jnp.where(first_ok if i == 0 else in_window, st, NEG_INF)
            sink = jnp.concatenate(
                [jnp.full((1, BLK), sinks_ref[kvh * Q_PER_KV + h], F32) for h in range(Q_PER_KV)], axis=1)
            m = jnp.maximum(jnp.max(st, axis=0, keepdims=True), sink)
            p = jnp.exp(st - m)
            denom = jnp.sum(p, axis=0, keepdims=True) + jnp.exp(sink - m)
            pvt = jnp.dot(vband_t, p.astype(BF16), preferred_element_type=F32)
            inv_denom = 1.0 / denom
            zt = jnp.concatenate([pvt[h * HEAD_DIM:(h + 1) * HEAD_DIM, h * BLK:(h + 1) * BLK]
                                  * inv_denom[:, h * BLK:(h + 1) * BLK] for h in range(Q_PER_KV)], axis=0)
            o_parts.append(jnp.transpose(zt))
        a = jnp.concatenate(o_parts, axis=1)
        mixed_s[r0:r0 + BLK, 0:ATTN_WIDTH] = _rms(a, ag_ref[...]).astype(BF16)

        u = _gelu(proj_s[r0:r0 + BLK, GU_OFF:GU_OFF + GMLP_WIDTH])
        vg = _gelu(proj_s[r0:r0 + BLK, GV_OFF:GV_OFF + GMLP_WIDTH])
        mu = jnp.mean(vg, axis=-1, keepdims=True)
        var = jnp.mean(jnp.square(vg - mu), axis=-1, keepdims=True)
        vln = (vg - mu) * lax.rsqrt(var + LN_EPS) * lng_ref[...] + lnb_ref[...]
        s_parts = []
        for pidx in range(GMLP_WIDTH // LANES):
            vcol = vln[:, pidx * LANES:(pidx + 1) * LANES]
            rhs = jnp.concatenate([jnp.where(lo_half, vcol, 0.0), jnp.where(lo_half, 0.0, vcol)],
                                  axis=0).astype(BF16)
            s_parts.append(jnp.dot(wsp[pidx], rhs, preferred_element_type=F32))
        sg = jnp.concatenate(s_parts, axis=1) + bsp_ref[...]
        g = u * sg
        mixed_s[r0:r0 + BLK, ATTN_WIDTH:D_MODEL] = _rms(g, gg_ref[...]).astype(BF16)

    for kvh in range(N_KV_HEADS):
        k4p_s[kvh] = k4_prev[kvh]
        v4p_s[kvh] = v4t_prev[kvh]
    h_new = x_ref[...] + jnp.dot(mixed_s[...], wout_ref[...], preferred_element_type=F32)
    h_ref[...] = h_new
    _route_tile(h_new, rg_ref, whl_ref, rb_ref, info_ref, wts_ref, cnt_ref, carry_s,
                (pl.program_id(0) == 0) & (j == 0))


def _mix(x2, sinks, g1, w_in, wsp, bsp, lng, lnb, ag, gg, w_out, rg, whl, rb, batch, seq):
    t = x2.shape[0]
    tm = TM_MIX
    nj = seq // tm
    full = lambda shape: pl.BlockSpec(shape, lambda b, j: (0,) * len(shape))
    return pl.pallas_call(
        _mix_kernel,
        grid=(batch, nj),
        in_specs=[
            pl.BlockSpec(memory_space=pltpu.SMEM),
            pl.BlockSpec((tm, D_MODEL), lambda b, j: (b * nj + j, 0)),
            full((1, D_MODEL)),
            full((D_MODEL, IN_PROJ)),
            full((GMLP_WIDTH // LANES, BLK, 2 * BLK)),
            full((BLK, GMLP_WIDTH)),
            full((1, GMLP_WIDTH)),
            full((1, GMLP_WIDTH)),
            full((1, ATTN_WIDTH)),
            full((1, GMLP_WIDTH)),
            full((D_MODEL, D_MODEL)),
            full((1, D_MODEL)),
            full((D_MODEL, 2 * LANES)),
            full((1, LANES)),
        ],
        out_specs=[
            pl.BlockSpec((tm, D_MODEL), lambda b, j: (b * nj + j, 0)),
            pl.BlockSpec((1, SUBLANES, tm), lambda b, j: (b * nj + j, 0, 0)),
            pl.BlockSpec((tm, LANES), lambda b, j: (b * nj + j, 0)),
            full((N_EXPERTS, LANES)),
        ],
        out_shape=[
            jax.ShapeDtypeStruct((t, D_MODEL), F32),
            jax.ShapeDtypeStruct((t // tm, SUBLANES, tm), F32),
            jax.ShapeDtypeStruct((t, LANES), F32),
            jax.ShapeDtypeStruct((N_EXPERTS, LANES), F32),
        ],
        scratch_shapes=[
            pltpu.VMEM((tm, IN_PROJ), F32),
            pltpu.VMEM((tm, D_MODEL), BF16),
            pltpu.VMEM((N_KV_HEADS, BLK, 2 * LANES), BF16),
            pltpu.VMEM((N_KV_HEADS, 2 * LANES, BLK), BF16),
            pltpu.VMEM((N_EXPERTS, LANES), F32),
        ],
        compiler_params=pltpu.CompilerParams(
            dimension_semantics=("arbitrary", "arbitrary"), vmem_limit_bytes=VMEM_LIMIT),
        name="mix",
    )(sinks, x2, g1, w_in, wsp, bsp, lng, lnb, ag, gg, w_out, rg, whl, rb)


INFO_W1, INFO_W2, INFO_E1, INFO_E2, INFO_R1, INFO_R2 = range(6)


def _route_tile(h, g_ref, whl_ref, b_ref, info_ref, wts_ref, cnt_ref, carry_s, first):
    tb = h.shape[0]

    @pl.when(first)
    def _():
        carry_s[...] = jnp.zeros_like(carry_s)

    hn = _rms(h, g_ref[...])
    hi = hn.astype(BF16)
    lo = (hn - hi.astype(F32)).astype(BF16)
    hi_both = jnp.dot(hi, whl_ref[...], preferred_element_type=F32)
    logits = (hi_both[:, 0:LANES]
              + (hi_both[:, LANES:2 * LANES] + jnp.dot(lo, whl_ref[:, 0:LANES], preferred_element_type=F32))
              + b_ref[...])
    lt = jnp.transpose(logits)
    row8 = lax.broadcasted_iota(I32, (SUBLANES, tb), 0)
    ninf = F32(-jnp.inf)

    def top_row(vals):
        best = jnp.max(vals, axis=0, keepdims=True)
        return best, jnp.min(jnp.where(vals == best, row8, SUBLANES), axis=0, keepdims=True)

    gl = jnp.where(row8 < N_GROUPS, lt[GROUP_ROW0:GROUP_ROW0 + SUBLANES], ninf)
    gmax, gidx = top_row(gl)
    g_w = 1.0 / jnp.sum(jnp.exp(gl - gmax), axis=0, keepdims=True)

    el = lt[EXPERT_ROW0:EXPERT_ROW0 + E_PER_GROUP]
    for g in range(1, N_GROUPS):
        el = jnp.where(gidx == g, lt[EXPERT_ROW0 + g * E_PER_GROUP:EXPERT_ROW0 + (g + 1) * E_PER_GROUP], el)
    v1, i1 = top_row(el)
    v2, i2 = top_row(jnp.where(row8 == i1, ninf, el))
    t2 = jnp.exp(v2 - v1)
    inv = 1.0 / (1.0 + t2)
    w1 = inv * g_w
    w2 = (t2 * inv) * g_w
    e1 = gidx * E_PER_GROUP + i1
    e2 = gidx * E_PER_GROUP + i2

    rowe = lax.broadcasted_iota(I32, (N_EXPERTS, tb), 0)
    sel1 = rowe == e1
    sel2 = rowe == e2
    oh = jnp.where(sel1 | sel2, 1.0, 0.0)
    rr = lax.broadcasted_iota(I32, (tb, tb), 0)
    cc = lax.broadcasted_iota(I32, (tb, tb), 1)
    triu = jnp.where(rr <= cc, 1.0, 0.0).astype(BF16)
    incl = jnp.dot(oh.astype(BF16), triu, preferred_element_type=F32)
    carry = carry_s[...]
    before = incl - oh + carry[:, 0:1]
    rank1 = jnp.sum(jnp.where(sel1, before, 0.0), axis=0, keepdims=True)
    rank2 = jnp.sum(jnp.where(sel2, before, 0.0), axis=0, keepdims=True)
    carry = carry + incl[:, tb - 1:tb]
    carry_s[...] = carry
    cnt_ref[...] = carry

    rows = (w1, w2, e1.astype(F32), e2.astype(F32), rank1, rank2)
    info = jnp.zeros((SUBLANES, tb), F32)
    for k, val in enumerate(rows):
        info = jnp.where(row8 == k, val, info)
    info_ref[0] = info
    row128 = lax.broadcasted_iota(I32, (LANES, tb), 0)
    wts_ref[...] = jnp.transpose(jnp.where(row128 == INFO_W1, w1, jnp.where(row128 == INFO_W2, w2, 0.0)))


def _dispatch_kernel(cnt_ref, off_ref, nt_ref, h_ref, g_ref, info_ref, offcol_ref, xs_hbm, dst_ref,
                     rows_s, zeros_s, dstv_s, dsts_s, sems, fill_sem, *, n_tiles):
    i = pl.program_id(0)
    n_steps = pl.num_programs(0)
    tb = h_ref.shape[0]
    slot = i % 2

    def row_copy(buf, t, dst_row):
        return pltpu.make_async_copy(rows_s.at[buf, pl.ds(t * CHUNKS, CHUNKS), :], xs_hbm.at[dst_row],
                                     sems.at[buf])

    def drain_tile(buf):
        for _ in range(2 * tb):
            row_copy(buf, 0, 0).wait()

    @pl.when(i >= 2)
    def _():
        drain_tile(slot)

    info = info_ref[0]
    rowe = lax.broadcasted_iota(I32, (N_EXPERTS, tb), 0).astype(F32)
    off = offcol_ref[...][:, 0:1]
    dst_rows = []
    for ke, kr in ((INFO_E1, INFO_R1), (INFO_E2, INFO_R2)):
        base = jnp.sum(jnp.where(rowe == info[ke:ke + 1, :], off, 0.0), axis=0, keepdims=True)
        dst_rows.append(base + info[kr:kr + 1, :])
    row8 = lax.broadcasted_iota(I32, (SUBLANES, tb), 0)
    dst_t = jnp.where(row8 == 0, dst_rows[0], jnp.where(row8 == 1, dst_rows[1], 0.0)).astype(I32)
    dstv_s[...] = dst_t
    dst_ref[0] = dst_t
    pltpu.sync_copy(dstv_s, dsts_s)

    for r0 in range(0, tb, DISP_ROWS):
        _store_token_tiles(rows_s.at[slot, pl.ds(r0 * CHUNKS, DISP_ROWS * CHUNKS), :],
                           _rms(h_ref[r0:r0 + DISP_ROWS, :], g_ref[...]))
        for t in range(r0, r0 + DISP_ROWS):
            row_copy(slot, t, dsts_s[0, t]).start(priority=0)
            row_copy(slot, t, dsts_s[1, t]).start(priority=1)

    @pl.when(i == n_steps - 1)
    def _():
        zeros_s[...] = jnp.zeros_like(zeros_s)
        half = TE // 2

        def fill_copy(pos, size):
            return pltpu.make_async_copy(zeros_s.at[pl.ds(0, size)], xs_hbm.at[pl.ds(pos, size)], fill_sem)

        def for_each_gap(action):
            for e in range(N_EXPERTS):
                n = cnt_ref[e]
                gap = (-n) & (TE - 1)
                pos = off_ref[e] + n
                size = half
                while size >= 1:
                    @pl.when((gap & size) != 0)
                    def _(pos=pos, size=size):
                        action(fill_copy(pos, size))
                    pos = pos + (gap & size)
                    size //= 2

        def tail_issue(k, c):
            fill_copy(k * half, half).start()
            return c

        def tail_wait(k, c):
            fill_copy(0, half).wait()
            return c

        for_each_gap(lambda cp: cp.start())
        lax.fori_loop(2 * nt_ref[0], 2 * n_tiles, tail_issue, 0)
        for_each_gap(lambda cp: cp.wait())
        lax.fori_loop(2 * nt_ref[0], 2 * n_tiles, tail_wait, 0)
        drain_tile(slot)

        @pl.when(i >= 1)
        def _():
            drain_tile(1 - slot)


def _dispatch(counts, offsets, nt, h, g, info, off_col, n_tiles):
    t = h.shape[0]
    tb = TB_ROUTE
    nsteps = t // tb
    full = lambda shape: pl.BlockSpec(shape, lambda i, *_: (0,) * len(shape))
    return pl.pallas_call(
        functools.partial(_dispatch_kernel, n_tiles=n_tiles),
        grid_spec=pltpu.PrefetchScalarGridSpec(
            num_scalar_prefetch=3,
            grid=(nsteps,),
            in_specs=[
                pl.BlockSpec((tb, D_MODEL), lambda i, *_: (i, 0)),
                full((1, D_MODEL)),
                pl.BlockSpec((1, SUBLANES, tb), lambda i, *_: (i, 0, 0)),
                full((N_EXPERTS, LANES)),
            ],
            out_specs=[
                pl.BlockSpec(memory_space=pl.ANY),
                pl.BlockSpec((1, 8, tb), lambda i, *_: (i, 0, 0)),
            ],
            scratch_shapes=[
                pltpu.VMEM((2, tb * CHUNKS, LANES), F32),
                pltpu.VMEM((TE // 2, CHUNKS, LANES), F32),
                pltpu.VMEM((8, tb), I32),
                pltpu.SMEM((8, tb), I32),
                pltpu.SemaphoreType.DMA((2,)),
                pltpu.SemaphoreType.DMA,
            ],
        ),
        out_shape=[
            jax.ShapeDtypeStruct((n_tiles * TE, CHUNKS, LANES), F32),
            jax.ShapeDtypeStruct((nsteps, 8, tb), I32),
        ],
        compiler_params=pltpu.CompilerParams(
            dimension_semantics=("arbitrary",), vmem_limit_bytes=VMEM_LIMIT),
        name="dispatch",
    )(counts, offsets, nt, h, g, info, off_col)


def _expert_kernel(te_ref, nt_ref, xs_hbm, wg_ref, wu_ref, wd_ref, y_ref, wgu_s, wdb_s, xbuf_s, sems):
    i = pl.program_id(0)
    nt = nt_ref[0]
    rows = TE * CHUNKS

    def tile_copy(tile, slot):
        return pltpu.make_async_copy(xs_hbm.at[pl.ds(tile * rows, rows), :], xbuf_s.at[slot], sems.at[slot])

    @pl.when(i == 0)
    def _():
        for k in range(X_RING - 1):
            @pl.when(k < nt)
            def _(k=k):
                tile_copy(k, k).start()

    ahead = i + (X_RING - 1)

    @pl.when(ahead < nt)
    def _():
        tile_copy(ahead, ahead % X_RING).start()

    @pl.when(i >= nt)
    def _():
        y_ref[...] = jnp.zeros_like(y_ref)

    @pl.when(i < nt)
    def _():
        slot = i % X_RING
        tile_copy(i, slot).wait()
        x_ref = xbuf_s.at[slot]
        prev = te_ref[jnp.maximum(i - 1, 0)]

        @pl.when((i == 0) | (te_ref[i] != prev))
        def _():
            wgu_s[:, 0:D_EXPERT] = wg_ref[0].astype(BF16)
            wgu_s[:, D_EXPERT:2 * D_EXPERT] = wu_ref[0].astype(BF16)
            wdb_s[...] = wd_ref[0].astype(BF16)

        x = _load_token_tiles(x_ref, TE).astype(BF16)
        ab = jnp.dot(x, wgu_s[...], preferred_element_type=F32)
        a = ab[:, 0:D_EXPERT]
        b = ab[:, D_EXPERT:2 * D_EXPERT]
        hid = (a * jax.nn.sigmoid(a)) * b
        y = jnp.dot(hid.astype(BF16), wdb_s[...], preferred_element_type=F32)
        _store_token_tiles(y_ref, y)


def _experts(te, nt, xs, w_gate, w_up, w_down, n_tiles):
    tile = lambda i, te, nt: (i, 0)
    wblk = lambda i, te, nt: (te[i], 0, 0)
    return pl.pallas_call(
        _expert_kernel,
        grid_spec=pltpu.PrefetchScalarGridSpec(
            num_scalar_prefetch=2,
            grid=(n_tiles,),
            in_specs=[
                pl.BlockSpec(memory_space=pl.ANY),
                pl.BlockSpec((1, D_MODEL, D_EXPERT), wblk),
                pl.BlockSpec((1, D_MODEL, D_EXPERT), wblk),
                pl.BlockSpec((1, D_EXPERT, D_MODEL), wblk),
            ],
            out_specs=pl.BlockSpec((TE * CHUNKS, LANES), tile),
            scratch_shapes=[
                pltpu.VMEM((D_MODEL, 2 * D_EXPERT), BF16),
                pltpu.VMEM((D_EXPERT, D_MODEL), BF16),
                pltpu.VMEM((X_RING, TE * CHUNKS, LANES), F32),
                pltpu.SemaphoreType.DMA((X_RING,)),
            ],
        ),
        out_shape=jax.ShapeDtypeStruct(xs.shape, F32),
        compiler_params=pltpu.CompilerParams(
            dimension_semantics=("arbitrary",), vmem_limit_bytes=VMEM_LIMIT),
        name="experts",
    )(te, nt, xs, w_gate, w_up, w_down)


def _combine_kernel(dst0_ref, dstn_ref, h_ref, wts_ref, g_ref, ys_hbm, out_ref, dsts_s, ybuf_s, sem):
    i = pl.program_id(0)
    n_steps = pl.num_programs(0)
    tb = h_ref.shape[0]
    slot = i % 2

    def row_copy(buf, k, t, src_row):
        return pltpu.make_async_copy(ys_hbm.at[src_row], ybuf_s.at[buf, k, pl.ds(t * CHUNKS, CHUNKS), :], sem)

    def issue_tile(buf):
        for t in range(tb):
            row_copy(buf, 0, t, dsts_s[0, t]).start(priority=0)
            row_copy(buf, 1, t, dsts_s[1, t]).start(priority=1)

    def drain_tile():
        for _ in range(2 * tb):
            row_copy(0, 0, 0, 0).wait()

    @pl.when(i == 0)
    def _():
        pltpu.sync_copy(dst0_ref.at[0], dsts_s)
        issue_tile(0)

    pltpu.sync_copy(dstn_ref.at[0], dsts_s)
    drain_tile()

    for r0 in range(0, tb, COMB_ROWS):
        rows = slice(r0, r0 + COMB_ROWS)
        part = [_load_token_tiles(ybuf_s.at[slot, k, pl.ds(r0 * CHUNKS, COMB_ROWS * CHUNKS), :], COMB_ROWS)
                for k in range(2)]
        w = wts_ref[rows, :]
        moe = w[:, INFO_W1:INFO_W1 + 1] * part[0] + w[:, INFO_W2:INFO_W2 + 1] * part[1]
        out_ref[rows, :] = _rms(h_ref[rows, :] + moe, g_ref[...])
        for t in range(r0, r0 + COMB_ROWS):
            row_copy(1 - slot, 0, t, dsts_s[0, t]).start(priority=0)
            row_copy(1 - slot, 1, t, dsts_s[1, t]).start(priority=1)

    @pl.when(i == n_steps - 1)
    def _():
        drain_tile()


def _combine(dst, h, wts, g, ys):
    t = h.shape[0]
    tb = TB_COMB
    n_steps = t // tb
    return pl.pallas_call(
        _combine_kernel,
        grid=(n_steps,),
        in_specs=[
            pl.BlockSpec((1, 8, tb), lambda i: (0, 0, 0)),
            pl.BlockSpec((1, 8, tb), lambda i: (jnp.minimum(i + 1, n_steps - 1), 0, 0)),
            pl.BlockSpec((tb, D_MODEL), lambda i: (i, 0)),
            pl.BlockSpec((tb, LANES), lambda i: (i, 0)),
            pl.BlockSpec((1, D_MODEL), lambda i: (0, 0)),
            pl.BlockSpec(memory_space=pl.ANY),
        ],
        out_specs=pl.BlockSpec((tb, D_MODEL), lambda i: (i, 0)),
        out_shape=jax.ShapeDtypeStruct((t, D_MODEL), F32),
        scratch_shapes=[
            pltpu.SMEM((8, tb), I32),
            pltpu.VMEM((2, 2, tb * CHUNKS, LANES), F32),
            pltpu.SemaphoreType.DMA,
        ],
        compiler_params=pltpu.CompilerParams(
            dimension_semantics=("arbitrary",), vmem_limit_bytes=VMEM_LIMIT),
        name="combine",
    )(dst, dst, h, wts, g, ys)


def _tile_tables(counts, n_tiles):
    per = (counts + (TE - 1)) // TE
    cum = jnp.cumsum(per)
    total = cum[-1]
    offsets = ((cum - per) * TE).astype(I32)
    idx = jnp.minimum(jnp.arange(n_tiles, dtype=I32), total - 1)
    te = jnp.minimum(jnp.sum((idx[:, None] >= cum[None, :]).astype(I32), axis=1), N_EXPERTS - 1)
    return offsets, te, total.reshape(1).astype(I32)


def kernel(x, mix_norm_g, w_in, attn_sinks, w_spatial, b_spatial, gmlp_ln_g, gmlp_ln_b, attn_out_g, gmlp_out_g, w_out, ffn_norm_g, w_group_router, b_group_router, w_expert_router, b_expert_router, w_gate, w_up, w_down, final_norm_g):
    batch, seq, _ = x.shape
    t = batch * seq
    depth = mix_norm_g.shape[0]
    assert depth == 1, "the final norm is fused into the last layer's combine kernel"
    n_tiles = (2 * t) // TE + N_EXPERTS
    h = x.reshape(t, D_MODEL)
    for l in range(depth):
        n_pairs = GMLP_WIDTH // LANES
        wsp = w_spatial[l].reshape(n_pairs, 2, BLK, BLK).transpose(0, 2, 1, 3).reshape(n_pairs, BLK, 2 * BLK)
        bsp = jnp.repeat(b_spatial[l].T, HEAD_DIM, axis=1)

        w_r = jnp.concatenate(
            [w_group_router[l], jnp.zeros((D_MODEL, EXPERT_ROW0 - N_GROUPS), F32),
             w_expert_router[l].transpose(1, 0, 2).reshape(D_MODEL, N_EXPERTS)], axis=1)
        w_r = jnp.pad(w_r, ((0, 0), (0, LANES - w_r.shape[1])))
        b_r = jnp.concatenate([b_group_router[l], jnp.zeros((EXPERT_ROW0 - N_GROUPS,), F32),
                               b_expert_router[l].reshape(-1)])
        b_r = jnp.pad(b_r, (0, LANES - b_r.shape[0]))[None]
        w_r_top = lax.bitcast_convert_type(
            lax.bitcast_convert_type(w_r, U32) & U32(0xFFFF0000), F32)
        w_r_hl = jnp.concatenate([w_r_top.astype(BF16), (w_r - w_r_top).astype(BF16)], axis=1)
        h, info, wts, cnt = _mix(h, attn_sinks[l], mix_norm_g[l][None], w_in[l].astype(BF16), wsp, bsp,
                                 gmlp_ln_g[l][None], gmlp_ln_b[l][None], attn_out_g[l][None],
                                 gmlp_out_g[l][None], w_out[l].astype(BF16), ffn_norm_g[l][None], w_r_hl, b_r,
                                 batch, seq)
        counts = cnt[:, 0].astype(I32)
        offsets, te, nt = _tile_tables(counts, n_tiles)
        off_col = jnp.broadcast_to(offsets.astype(F32)[:, None], (N_EXPERTS, LANES))
        xs, dst = _dispatch(counts, offsets, nt, h, ffn_norm_g[l][None], info, off_col, n_tiles)
        ys = _experts(te, nt, xs.reshape(-1, LANES), w_gate[l], w_up[l], w_down[l], n_tiles)
        h = _combine(dst, h, wts, final_norm_g[None], ys.reshape(xs.shape))
    return h.reshape(batch, seq, D_MODEL)
```
